```python
import math, functools
import jax, jax.numpy as jnp
from jax import lax
import numpy as np

D_MODEL = 2048
BATCH = 8
SEQ = 2048
DEPTH = 2
DEC_BATCH = 128
DEC_SEQ = 8
PAST_LEN = 16384
PAGE_SIZE = 128

A_HEADS = 8
A_NOPE = 128
A_ROPE = 64
A_VDIM = 128
Q_LORA = 384
KV_LORA = 256
ROPE_THETA = 10000.0
A_WIDTH = A_HEADS * A_VDIM
Q_BLOCK = 128
M_HEADS = 8
M_HEADDIM = 64
M_WIDTH = M_HEADS * M_HEADDIM
M_GROUPS = 2
M_STATE = 128
M_CONV = 4
M_CONV_CH = M_WIDTH + 2 * M_GROUPS * M_STATE
G_HEADS = 4
G_DK = 128
G_DV = 128
G_WIDTH = G_HEADS * G_DV
G_CONV = 4
G_CONV_CH = 2 * G_HEADS * G_DK + G_WIDTH
CHUNK = 64
D_MIX = A_WIDTH + M_WIDTH + G_WIDTH
IN_SIZES = (Q_LORA, KV_LORA, A_ROPE, M_WIDTH, M_CONV_CH, M_HEADS, G_CONV_CH, G_WIDTH, G_HEADS, G_HEADS)
D_IN = Q_LORA + KV_LORA + A_ROPE + M_WIDTH + M_CONV_CH + M_HEADS + G_CONV_CH + G_WIDTH + 2 * G_HEADS
P_HEADS = 8
P_NKEYS = 128
P_NEXPERTS = P_NKEYS * P_NKEYS
P_DKEY = 256
P_TOPK = 16
P_BLOCK = 128
EPS = 1e-6
F32 = jnp.float32

kernel_name = 'hymba_mla_ssd_gdn_peer_step'


def rmsnorm(x, g):
    xf = x.astype(F32)
    y = xf * lax.rsqrt(jnp.mean(xf * xf, axis=-1, keepdims=True) + EPS)
    return (y * g.astype(F32)).astype(x.dtype)


def l2norm(x):
    return x * lax.rsqrt(jnp.sum(x * x, axis=-1, keepdims=True) + EPS)


def rope(x, pos):
    half = x.shape[-1] // 2
    inv = ROPE_THETA ** (-jnp.arange(half, dtype=F32) / half)
    ang = pos.astype(F32)[:, None] * inv[None, :]
    cos = jnp.cos(ang)[None, :, None, :]
    sin = jnp.sin(ang)[None, :, None, :]
    x1 = x[..., :half].astype(F32)
    x2 = x[..., half:].astype(F32)
    return jnp.concatenate([x1 * cos - x2 * sin, x2 * cos + x1 * sin], axis=-1).astype(x.dtype)


def causal_conv(x, buf, w, b):
    width = w.shape[0]
    L = x.shape[1]
    xp = jnp.concatenate([buf.astype(x.dtype), x], axis=1)
    y = xp[:, 0:L] * w[0]
    for k in range(1, width):
        y = y + xp[:, k:k + L] * w[k]
    if b is not None:
        y = y + b
    return jax.nn.silu(y), xp[:, L:]


def chunk_len(L):
    return CHUNK if L % CHUNK == 0 else L


def mla_attend_prompt(q_nope, q_rope, c, k_rope, k_raw, k_inv, w_kb, w_vb, kn_nope):
    Bsz, S, H, _ = q_nope.shape
    k_nope = k_raw * k_inv[..., None] * kn_nope
    v = jnp.einsum('bsr,rhd->bshd', c, w_vb)
    nb = S // Q_BLOCK
    qn = q_nope.reshape(Bsz, nb, Q_BLOCK, H, A_NOPE).swapaxes(0, 1)
    qr = q_rope.reshape(Bsz, nb, Q_BLOCK, H, A_ROPE).swapaxes(0, 1)
    kpos = jnp.arange(S)
    scale = (A_NOPE + A_ROPE) ** -0.5

    def block(args):
        qn_b, qr_b, start = args
        s = jnp.einsum('bqhd,bkhd->bhqk', qn_b, k_nope) + jnp.einsum('bqhd,bkd->bhqk', qr_b, k_rope)
        qpos = start + jnp.arange(Q_BLOCK)
        s = jnp.where(kpos[None, :] <= qpos[:, None], s.astype(F32) * scale, -jnp.inf)
        p = jax.nn.softmax(s, axis=-1).astype(v.dtype)
        return jnp.einsum('bhqk,bkhd->bqhd', p, v)

    o = lax.map(block, (qn, qr, jnp.arange(nb) * Q_BLOCK))
    return o.swapaxes(0, 1).reshape(Bsz, S, H * A_VDIM)


def mla_attend_sample(cache_ckv, cache_krope, cache_kscale, page_table, li,
                      q_nope, q_rope, c, k_rope, k_raw, k_inv, w_kb, w_vb, kn_nope):
    Bd, L, H, _ = q_nope.shape
    past = page_table.shape[1] * PAGE_SIZE
    q_abs = jnp.einsum('blhd,rhd->blhr', q_nope * kn_nope, w_kb)
    scale = (A_NOPE + A_ROPE) ** -0.5
    mask = jnp.concatenate([jnp.ones((L, past), bool), jnp.tril(jnp.ones((L, L), bool))], axis=1)

    def one(args):
        qa, qr, cn, krn, ksn, pt = args
        cc = jnp.concatenate([cache_ckv[li, pt].reshape(past, KV_LORA), cn.astype(cache_ckv.dtype)], axis=0)
        kr = jnp.concatenate([cache_krope[li, pt].reshape(past, A_ROPE), krn.astype(cache_krope.dtype)], axis=0)
        ks = jnp.concatenate([cache_kscale[li, pt].reshape(past, A_HEADS), ksn.astype(cache_kscale.dtype)], axis=0)
        s = jnp.einsum('qhr,kr->hqk', qa, cc) * ks.T[:, None, :] + jnp.einsum('qhd,kd->hqk', qr, kr)
        s = jnp.where(mask, s.astype(F32) * scale, -jnp.inf)
        p = jax.nn.softmax(s, axis=-1).astype(cc.dtype)
        o_lat = jnp.einsum('hqk,kr->qhr', p, cc)
        return jnp.einsum('qhr,rhd->qhd', o_lat, w_vb).reshape(L, H * A_VDIM)

    return lax.map(one, (q_abs, q_rope, c, k_rope, k_inv, page_table))


def ssd(x, dt, A, Bm, Cm, S0):
    Bsz, L, H, P = x.shape
    c = chunk_len(L)
    nc = L // c
    rep = H // M_GROUPS
    Bh = jnp.repeat(Bm, rep, axis=2)
    Ch = jnp.repeat(Cm, rep, axis=2)

    def to_chunks(t):
        return t.reshape(Bsz, nc, c, *t.shape[2:]).swapaxes(0, 1)

    tri = jnp.tril(jnp.ones((c, c), bool))

    def step(S, inp):
        xc, dtc, ac, Bc, Cc = inp
        cum = jnp.cumsum(ac, axis=1)
        diff = cum[:, :, None, :] - cum[:, None, :, :]
        Lmat = jnp.exp(jnp.where(tri[None, :, :, None], diff, -jnp.inf))
        sc = jnp.einsum('bihn,bjhn->bijh', Cc, Bc) * Lmat
        y = jnp.einsum('bijh,bjh,bjhp->bihp', sc, dtc, xc)
        y = y + jnp.einsum('bihn,bhpn->bihp', Cc, S) * jnp.exp(cum)[..., None]
        to_end = jnp.exp(cum[:, -1:, :] - cum) * dtc
        S = S * jnp.exp(cum[:, -1])[:, :, None, None] + jnp.einsum('bjh,bjhp,bjhn->bhpn', to_end, xc, Bc)
        return S, y

    S, ys = lax.scan(step, S0, (to_chunks(x), to_chunks(dt), to_chunks(dt * A), to_chunks(Bh), to_chunks(Ch)))
    return ys.swapaxes(0, 1).reshape(Bsz, L, H, P), S


def mamba_mixer(z, xbc, dt_raw, conv_buf, S0, conv_w, conv_b, dt_bias, A_log, Dskip, norm_g):
    Bsz, L, _ = xbc.shape
    xbc, new_buf = causal_conv(xbc, conv_buf, conv_w, conv_b)
    xbc = xbc.astype(F32)
    xs = xbc[..., :M_WIDTH].reshape(Bsz, L, M_HEADS, M_HEADDIM)
    Bm = xbc[..., M_WIDTH:M_WIDTH + M_GROUPS * M_STATE].reshape(Bsz, L, M_GROUPS, M_STATE)
    Cm = xbc[..., M_WIDTH + M_GROUPS * M_STATE:].reshape(Bsz, L, M_GROUPS, M_STATE)
    dt = jax.nn.softplus(dt_raw.astype(F32) + dt_bias.astype(F32))
    A = -jnp.exp(A_log.astype(F32))
    y, S = ssd(xs, dt, A, Bm, Cm, S0.astype(F32))
    y = y + xs * Dskip.astype(F32)[:, None]
    y = rmsnorm(y.reshape(Bsz, L, M_WIDTH) * jax.nn.silu(z.astype(F32)), norm_g)
    return y.astype(z.dtype), new_buf, S.astype(S0.dtype)


def gated_delta(q, k, v, g, beta, S0):
    Bsz, L, H, dk = q.shape
    dv = v.shape[-1]
    c = chunk_len(L)
    nc = L // c

    def ch(t):
        t = t.reshape(Bsz, nc, c, H, *t.shape[3:])
        return jnp.moveaxis(t, (1, 3), (0, 2))

    qc, kc, vc, gc0, bc = ch(q), ch(k), ch(v), ch(g), ch(beta)
    gcum = jnp.cumsum(gc0, axis=-1)
    diff = gcum[..., :, None] - gcum[..., None, :]
    incl = jnp.tril(jnp.ones((c, c), bool))
    strict = jnp.tril(jnp.ones((c, c), bool), -1)
    decay = jnp.exp(jnp.where(incl, diff, -jnp.inf))
    kb = kc * bc[..., None]
    Amat = jnp.where(strict, jnp.einsum('...id,...jd->...ij', kb, kc) * decay, 0.0)
    eye = jnp.eye(c, dtype=F32)
    T = lax.linalg.triangular_solve(Amat + eye, jnp.broadcast_to(eye, Amat.shape),
                                    left_side=True, lower=True, unit_diagonal=True)
    U = T @ (vc * bc[..., None])
    W = T @ (kb * jnp.exp(gcum)[..., None])
    attn = jnp.where(incl, jnp.einsum('...id,...jd->...ij', qc, kc) * decay, 0.0)

    def step(S, inp):
        q_i, k_i, U_i, W_i, g_i, a_i = inp
        v_new = U_i - W_i @ S
        o = (q_i * jnp.exp(g_i)[..., None]) @ S + a_i @ v_new
        g_last = g_i[..., -1]
        S = S * jnp.exp(g_last)[..., None, None] + jnp.einsum(
            'bhjd,bhje->bhde', k_i * jnp.exp(g_last[..., None] - g_i)[..., None], v_new)
        return S, o

    S, o = lax.scan(step, S0, (qc, kc, U, W, gcum, attn))
    return jnp.moveaxis(o, (0, 2), (1, 3)).reshape(Bsz, L, H, dv), S


def gdn_mixer(qkv, z, b_raw, a_raw, conv_buf, S0, conv_w, dt_bias, A_log, norm_g):
    Bsz, L, _ = qkv.shape
    qkv, new_buf = causal_conv(qkv, conv_buf, conv_w, None)
    qkv = qkv.astype(F32)
    nq = G_HEADS * G_DK
    q = l2norm(qkv[..., :nq].reshape(Bsz, L, G_HEADS, G_DK)) * (G_DK ** -0.5)
    k = l2norm(qkv[..., nq:2 * nq].reshape(Bsz, L, G_HEADS, G_DK))
    v = qkv[..., 2 * nq:].reshape(Bsz, L, G_HEADS, G_DV)
    beta = jax.nn.sigmoid(b_raw.astype(F32))
    g = -jnp.exp(A_log.astype(F32)) * jax.nn.softplus(a_raw.astype(F32) + dt_bias.astype(F32))
    o, S = gated_delta(q, k, v, g, beta, S0.astype(F32))
    o = rmsnorm(o, norm_g) * jax.nn.silu(z.astype(F32).reshape(Bsz, L, G_HEADS, G_DV))
    return o.reshape(Bsz, L, G_WIDTH).astype(z.dtype), new_buf, S.astype(S0.dtype)


def peer(h, wq, keys, u, v):
    Bsz, L, D = h.shape
    T = Bsz * L
    pad = (-T) % P_BLOCK
    t = jnp.pad(h.reshape(T, D), ((0, pad), (0, 0)))
    nb = (T + pad) // P_BLOCK

    def block(tb):
        q = (tb @ wq).reshape(P_BLOCK, P_HEADS, 2, P_DKEY // 2)
        s = jnp.einsum('thcd,hcnd->thcn', q, keys).astype(F32)
        s_top, i_top = lax.top_k(s, P_TOPK)
        cand = s_top[:, :, 0, :, None] + s_top[:, :, 1, None, :]
        cand_idx = i_top[:, :, 0, :, None] * P_NKEYS + i_top[:, :, 1, None, :]
        best, sel = lax.top_k(cand.reshape(P_BLOCK, P_HEADS, P_TOPK * P_TOPK), P_TOPK)
        idx = jnp.take_along_axis(cand_idx.reshape(P_BLOCK, P_HEADS, P_TOPK * P_TOPK), sel, axis=-1)
        gate = jax.nn.softmax(best, axis=-1)
        pre = jnp.einsum('td,thkd->thk', tb, u[idx]).astype(F32)
        act = (jax.nn.gelu(pre, approximate=False) * gate).astype(tb.dtype)
        return jnp.einsum('thk,thkd->td', act, v[idx])

    out = lax.map(block, t.reshape(nb, P_BLOCK, D))
    return out.reshape(nb * P_BLOCK, D)[:T].reshape(Bsz, L, D)


def layer(x, pos, attend, m_buf, m_S, g_buf, g_S, w):
    (ln1, w_in, w_out, ln2, q_a_norm, w_qb, kv_a_norm, w_kb, w_vb, qn_nope, qn_rope, kn_nope, kn_rope,
     m_conv_w, m_conv_b, m_dt_bias, m_A_log, m_D, m_norm,
     g_conv_w, g_dt_bias, g_A_log, g_norm, p_wq, p_keys, p_u, p_v) = w
    Bsz, L, _ = x.shape
    h = rmsnorm(x, ln1)
    proj = h @ w_in
    splits = np.cumsum(IN_SIZES)[:-1].tolist()
    q_lat, kv_lat, kr, m_z, m_xbc, m_dt, g_qkv, g_z, g_b, g_a = jnp.split(proj, splits, axis=-1)
    q = (rmsnorm(q_lat, q_a_norm) @ w_qb).reshape(Bsz, L, A_HEADS, A_NOPE + A_ROPE)
    q_nope = rmsnorm(q[..., :A_NOPE], qn_nope)
    q_rope = rope(rmsnorm(q[..., A_NOPE:], qn_rope), pos)
    c = rmsnorm(kv_lat, kv_a_norm)
    k_rope = rope(rmsnorm(kr, kn_rope)[:, :, None, :], pos)[:, :, 0, :]
    k_raw = jnp.einsum('blr,rhd->blhd', c, w_kb)
    kf = k_raw.astype(F32)
    k_inv = lax.rsqrt(jnp.mean(kf * kf, axis=-1) + EPS).astype(x.dtype)
    a_out = attend(q_nope, q_rope, c, k_rope, k_raw, k_inv, w_kb, w_vb, kn_nope)
    m_out, m_buf, m_S = mamba_mixer(m_z, m_xbc, m_dt, m_buf, m_S, m_conv_w, m_conv_b,
                                    m_dt_bias, m_A_log, m_D, m_norm)
    g_out, g_buf, g_S = gdn_mixer(g_qkv, g_z, g_b, g_a, g_buf, g_S, g_conv_w, g_dt_bias, g_A_log, g_norm)
    mix = jnp.concatenate([a_out.astype(x.dtype), m_out, g_out], axis=-1)
    x = x + mix @ w_out
    x = x + peer(rmsnorm(x, ln2), p_wq, p_keys, p_u, p_v)
    return x, (c, k_rope, k_inv, m_S, m_buf, g_S, g_buf)


def setup_inputs(seed: int = 0) -> dict:
    key = jax.random.key(seed)
    keys = iter(jax.random.split(key, 64))

    def nrm(shape, scale):
        return jax.random.normal(next(keys), shape, F32) * scale

    def gain(shape):
        return 1.0 + 0.01 * jax.random.normal(next(keys), shape, F32)

    def unif(shape, lo, hi):
        return jax.random.uniform(next(keys), shape, F32, lo, hi)

    def inv_softplus(d):
        return d + jnp.log(-jnp.expm1(-d))

    n_pages = PAST_LEN // PAGE_SIZE
    n_used = DEC_BATCH * n_pages
    n_pool = n_used + max(1, n_used // 4)
    page_table = jax.random.permutation(next(keys), n_pool)[:n_used].reshape(DEC_BATCH, n_pages).astype(jnp.int32)
    dt_m = jnp.exp(unif((DEPTH, M_HEADS), math.log(1e-3), math.log(1e-1)))
    dt_g = jnp.exp(unif((DEPTH, G_HEADS), math.log(1e-3), math.log(1e-1)))
    return {
        'x_prompt': nrm((BATCH, SEQ, D_MODEL), 1.0),
        'x_sample': nrm((DEC_BATCH, DEC_SEQ, D_MODEL), 1.0),
        'cache_ckv': nrm((DEPTH, n_pool, PAGE_SIZE, KV_LORA), 1.0),
        'cache_krope': nrm((DEPTH, n_pool, PAGE_SIZE, A_ROPE), 1.0),
        'cache_kscale': unif((DEPTH, n_pool, PAGE_SIZE, A_HEADS), 0.5, 1.5),
        'state_ssm': nrm((DEPTH, DEC_BATCH, M_HEADS, M_HEADDIM, M_STATE), 0.1),
        'state_ssm_conv': nrm((DEPTH, DEC_BATCH, M_CONV - 1, M_CONV_CH), 1.0),
        'state_gdn': nrm((DEPTH, DEC_BATCH, G_HEADS, G_DK, G_DV), 0.1),
        'state_gdn_conv': nrm((DEPTH, DEC_BATCH, G_CONV - 1, G_CONV_CH), 1.0),
        'page_table': page_table,
        'ln1': gain((DEPTH, D_MODEL)),
        'w_in': nrm((DEPTH, D_MODEL, D_IN), D_MODEL ** -0.5),
        'w_out': nrm((DEPTH, D_MIX, D_MODEL), D_MIX ** -0.5),
        'ln2': gain((DEPTH, D_MODEL)),
        'q_a_norm': gain((DEPTH, Q_LORA)),
        'w_qb': nrm((DEPTH, Q_LORA, A_HEADS * (A_NOPE + A_ROPE)), Q_LORA ** -0.5),
        'kv_a_norm': gain((DEPTH, KV_LORA)),
        'w_kb': nrm((DEPTH, KV_LORA, A_HEADS, A_NOPE), KV_LORA ** -0.5),
        'w_vb': nrm((DEPTH, KV_LORA, A_HEADS, A_VDIM), KV_LORA ** -0.5),
        'qn_nope': gain((DEPTH, A_NOPE)),
        'qn_rope': gain((DEPTH, A_ROPE)),
        'kn_nope': gain((DEPTH, A_NOPE)),
        'kn_rope': gain((DEPTH, A_ROPE)),
        'm_conv_w': nrm((DEPTH, M_CONV, M_CONV_CH), M_CONV ** -0.5),
        'm_conv_b': nrm((DEPTH, M_CONV_CH), 0.01),
        'm_dt_bias': inv_softplus(dt_m),
        'm_A_log': jnp.log(unif((DEPTH, M_HEADS), 1.0, 16.0)),
        'm_D': gain((DEPTH, M_HEADS)),
        'm_norm': gain((DEPTH, M_WIDTH)),
        'g_conv_w': nrm((DEPTH, G_CONV, G_CONV_CH), G_CONV ** -0.5),
        'g_dt_bias': inv_softplus(dt_g),
        'g_A_log': jnp.log(unif((DEPTH, G_HEADS), 1.0, 16.0)),
        'g_norm': gain((DEPTH, G_DV)),
        'p_wq': nrm((DEPTH, D_MODEL, P_HEADS * P_DKEY), D_MODEL ** -0.5),
        'p_keys': nrm((DEPTH, P_HEADS, 2, P_NKEYS, P_DKEY // 2), (P_DKEY // 2) ** -0.5),
        'p_u': nrm((DEPTH, P_NEXPERTS, D_MODEL), D_MODEL ** -0.5),
        'p_v': nrm((DEPTH, P_NEXPERTS, D_MODEL), (P_HEADS * P_TOPK) ** -0.5),
    }


def reference(x_prompt, x_sample, cache_ckv, cache_krope, cache_kscale, state_ssm, state_ssm_conv,
              state_gdn, state_gdn_conv, page_table, ln1, w_in, w_out, ln2, q_a_norm, w_qb, kv_a_norm,
              w_kb, w_vb, qn_nope, qn_rope, kn_nope, kn_rope, m_conv_w, m_conv_b, m_dt_bias, m_A_log,
              m_D, m_norm, g_conv_w, g_dt_bias, g_A_log, g_norm, p_wq, p_keys, p_u, p_v):
    stacked = (ln1, w_in, w_out, ln2, q_a_norm, w_qb, kv_a_norm, w_kb, w_vb, qn_nope, qn_rope,
               kn_nope, kn_rope, m_conv_w, m_conv_b, m_dt_bias, m_A_log, m_D, m_norm,
               g_conv_w, g_dt_bias, g_A_log, g_norm, p_wq, p_keys, p_u, p_v)
    B, S, _ = x_prompt.shape
    L = x_sample.shape[1]
    dt = x_prompt.dtype
    pos_p = jnp.arange(S)
    pos_s = page_table.shape[1] * PAGE_SIZE + jnp.arange(L)
    xp, xs = x_prompt, x_sample
    new_p = [[] for _ in range(7)]
    new_s = [[] for _ in range(7)]
    for i in range(DEPTH):
        w = tuple(a[i] for a in stacked)
        xp, st_p = layer(xp, pos_p, mla_attend_prompt,
                         jnp.zeros((B, M_CONV - 1, M_CONV_CH), dt),
                         jnp.zeros((B, M_HEADS, M_HEADDIM, M_STATE), dt),
                         jnp.zeros((B, G_CONV - 1, G_CONV_CH), dt),
                         jnp.zeros((B, G_HEADS, G_DK, G_DV), dt), w)
        attend_s = functools.partial(mla_attend_sample, cache_ckv, cache_krope, cache_kscale, page_table, i)
        xs, st_s = layer(xs, pos_s, attend_s, state_ssm_conv[i], state_ssm[i],
                         state_gdn_conv[i], state_gdn[i], w)
        for lst, val in zip(new_p, st_p):
            lst.append(val)
        for lst, val in zip(new_s, st_s):
            lst.append(val)
    p_ckv, p_krope, p_kscale, p_ssm, p_ssm_conv, p_gdn, p_gdn_conv = [jnp.stack(v) for v in new_p]
    s_ckv, s_krope, s_kscale, s_ssm, s_ssm_conv, s_gdn, s_gdn_conv = [jnp.stack(v) for v in new_s]
    return (xp, xs, p_ckv, p_krope, p_kscale, p_ssm, p_ssm_conv, p_gdn, p_gdn_conv,
            s_ckv, s_krope, s_kscale, s_ssm, s_ssm_conv, s_gdn, s_gdn_conv)
```

```python
import functools
import math

import jax
import jax.numpy as jnp
import numpy as np
from jax import lax
from jax.experimental import pallas as pl
from jax.experimental.pallas import tpu as pltpu

F32 = jnp.float32
BF16 = jnp.bfloat16
EPS = 1e-6

A_HEADS, A_NOPE, A_ROPE, A_VDIM = 8, 128, 64, 128
Q_LORA, KV_LORA = 384, 256
ROPE_THETA = 10000.0
M_HEADS, M_HEADDIM, M_WIDTH, M_GROUPS, M_STATE, M_CONV = 8, 64, 512, 2, 128, 4
M_CONV_CH = M_WIDTH + 2 * M_GROUPS * M_STATE
G_HEADS, G_DK, G_DV, G_WIDTH, G_CONV = 4, 128, 128, 512, 4
G_CONV_CH = 2 * G_HEADS * G_DK + G_WIDTH
IN_SIZES = (Q_LORA, KV_LORA, A_ROPE, M_WIDTH, M_CONV_CH, M_HEADS, G_CONV_CH, G_WIDTH, G_HEADS, G_HEADS)
P_HEADS, P_NKEYS, P_DKEY, P_TOPK = 8, 128, 256, 16
PAGE_SIZE = 128
CHUNK = 64
LANES = 128
SMALL_W = LANES
G_B_COL, G_A_COL = M_HEADS, M_HEADS + G_HEADS
VMEM_LIMIT = 56 * 1024 * 1024
ATT_SCALE = (A_NOPE + A_ROPE) ** -0.5


def _mm(a, b):
    return jnp.dot(a.astype(BF16), b.astype(BF16), preferred_element_type=F32)


def _mm_nt(a, b):
    return lax.dot_general(a.astype(BF16), b.astype(BF16), (((1,), (1,)), ((), ())),
                           preferred_element_type=F32)


def _mm_tn(a, b):
    return lax.dot_general(a.astype(BF16), b.astype(BF16), (((0,), (0,)), ((), ())),
                           preferred_element_type=F32)


def _hdot(a, b):
    return jnp.dot(a, b, precision=lax.Precision.HIGHEST, preferred_element_type=F32)


def _rms(x, g):
    return x * lax.rsqrt(jnp.mean(x * x, axis=-1, keepdims=True) + EPS) * g


def _softplus(x):
    return jnp.maximum(x, 0.0) + jnp.log1p(jnp.exp(-jnp.abs(x)))


def _silu(x):
    return x * jax.nn.sigmoid(x)


def _const_spec(a):
    nd = a.ndim
    return pl.BlockSpec(a.shape, lambda *_: (0,) * nd, pipeline_mode=pl.Buffered(1))


def _params(sem):
    return pltpu.CompilerParams(dimension_semantics=sem, vmem_limit_bytes=VMEM_LIMIT)


def _rowwise(body, rows, consts, outs, tm, name):
    T = rows[0].shape[0]
    assert T % tm == 0
    nr, nc = len(rows), len(consts)

    def kern(*refs):
        body(refs[:nr], refs[nr:nr + nc], refs[nr + nc:])

    in_specs = []
    for a in rows:
        assert a.shape[0] % tm == 0
        per = a.shape[0] // tm
        if a.shape[0] == T:
            in_specs.append(pl.BlockSpec((tm, a.shape[1]), lambda i: (i, 0)))
        else:
            in_specs.append(pl.BlockSpec((tm, a.shape[1]), lambda i, per=per: (i % per, 0)))
    in_specs += [_const_spec(a) for a in consts]
    out_specs = [pl.BlockSpec((tm, c), lambda i: (i, 0)) for c, _ in outs]
    out_shape = [jax.ShapeDtypeStruct((T, c), dt) for c, dt in outs]
    return pl.pallas_call(kern, grid=(T // tm,), in_specs=in_specs, out_specs=out_specs,
                          out_shape=out_shape, compiler_params=_params(("parallel",)),
                          name=name)(*rows, *consts)


def _in_proj_body(rows, consts, outs):
    x = rows[0][...]
    h = _rms(x, consts[0][...]).astype(BF16)
    for w_ref, o_ref in zip(consts[1:], outs):
        o_ref[...] = jnp.dot(h, w_ref[...], preferred_element_type=F32)


def _in_proj(x, ln1, ws, tm):
    outs = [(w.shape[1], F32) for w in ws]
    return _rowwise(_in_proj_body, [x], [ln1] + list(ws), outs, tm, "in_proj")


def _rope_rot(x, cos, sin_signed):
    w = x.shape[-1]
    lane = lax.broadcasted_iota(jnp.int32, x.shape, 1)
    first = (lane & (A_ROPE - 1)) < (A_ROPE // 2)
    rolled = jnp.where(first, pltpu.roll(x, w - A_ROPE // 2, 1), pltpu.roll(x, A_ROPE // 2, 1))
    return x * cos + rolled * sin_signed


def _mla_common(q_lat, kvr, cos, sin, qan, w_qn, w_qr, qn_rope, kvan, kn_rope, w_kb):
    ql = _rms(q_lat, qan).astype(BF16)
    qn = jnp.dot(ql, w_qn, preferred_element_type=F32)
    qr = jnp.dot(ql, w_qr, preferred_element_type=F32)
    wq = qr.shape[1]
    r = lax.broadcasted_iota(jnp.int32, (wq, wq), 0) >> 6
    c = lax.broadcasted_iota(jnp.int32, (wq, wq), 1) >> 6
    head_ones = (r == c).astype(F32)
    ss = _hdot(qr * qr, head_ones)
    qr = qr * lax.rsqrt(ss * (1.0 / A_ROPE) + EPS) * qn_rope
    reps = wq // LANES
    qr = _rope_rot(qr, jnp.tile(cos, (1, reps)), jnp.tile(sin, (1, reps)))
    kv = kvr[:, :KV_LORA]
    kr = kvr[:, KV_LORA:]
    cvec = _rms(kv, kvan)
    krn = kr * lax.rsqrt(jnp.sum(kr * kr, axis=-1, keepdims=True) * (1.0 / A_ROPE) + EPS) * kn_rope
    krope = _rope_rot(krn, cos, sin)
    kraw = jnp.dot(cvec.astype(BF16), w_kb, preferred_element_type=F32)
    return qn, qr, cvec, krope, kraw


def _k_inv_heads(kraw):
    tm = kraw.shape[0]
    lane8 = lax.broadcasted_iota(jnp.int32, (tm, A_HEADS), 1)
    kinv8 = jnp.zeros((tm, A_HEADS), F32)
    invs = []
    for h in range(A_HEADS):
        blk = kraw[:, h * A_NOPE:(h + 1) * A_NOPE]
        inv = lax.rsqrt(jnp.mean(blk * blk, axis=-1, keepdims=True) + EPS)
        invs.append(inv)
        kinv8 = jnp.where(lane8 == h, inv, kinv8)
    return invs, kinv8


def _mla_prompt_kernel(qlat_ref, kvr_ref, cos_ref, sin_ref, qan_ref, wqn_ref, wqr_ref, qnn_ref, qnr_ref,
                       kvan_ref, knr_ref, wkb_ref, knn_ref, wvb_ref,
                       qcat_ref, kcat_ref, v_ref, c_ref, krope_ref, kinv_ref):
    qn, qr, cvec, krope, kraw = _mla_common(
        qlat_ref[...], kvr_ref[...], cos_ref[...], sin_ref[...], qan_ref[...], wqn_ref[...], wqr_ref[...],
        qnr_ref[...], kvan_ref[...], knr_ref[...], wkb_ref[...])
    invs, kinv8 = _k_inv_heads(kraw)
    kr64 = krope[:, :A_ROPE]
    for h in range(A_HEADS):
        qn_h = _rms(qn[:, h * A_NOPE:(h + 1) * A_NOPE], qnn_ref[...])
        qcat_ref[h] = jnp.concatenate([qn_h, qr[:, h * A_ROPE:(h + 1) * A_ROPE]], axis=-1).astype(BF16)
        kn_h = kraw[:, h * A_NOPE:(h + 1) * A_NOPE] * invs[h] * knn_ref[...]
        kcat_ref[h] = jnp.concatenate([kn_h, kr64], axis=-1).astype(BF16)
    v_ref[...] = jnp.dot(cvec.astype(BF16), wvb_ref[...], preferred_element_type=F32).astype(BF16)
    c_ref[...] = cvec
    krope_ref[...] = kr64
    kinv_ref[...] = kinv8


def _mla_sample_kernel(qlat_ref, kvr_ref, cos_ref, sin_ref, qan_ref, wqn_ref, wqr_ref, qnn_ref, qnr_ref,
                       kvan_ref, knr_ref, wkb_ref, knn_ref, wkbt_ref,
                       qa_ref, qr_ref, c_ref, krope_ref, kinv_ref):
    qn, qr, cvec, krope, kraw = _mla_common(
        qlat_ref[...], kvr_ref[...], cos_ref[...], sin_ref[...], qan_ref[...], wqn_ref[...], wqr_ref[...],
        qnr_ref[...], kvan_ref[...], knr_ref[...], wkb_ref[...])
    _, kinv8 = _k_inv_heads(kraw)
    for h in range(A_HEADS):
        qn_h = _rms(qn[:, h * A_NOPE:(h + 1) * A_NOPE], qnn_ref[...]) * knn_ref[...]
        qa_ref[:, h * KV_LORA:(h + 1) * KV_LORA] = jnp.dot(
            qn_h.astype(BF16), wkbt_ref[h], preferred_element_type=F32).astype(BF16)
    qr_ref[...] = qr.astype(BF16)
    c_ref[...] = cvec
    krope_ref[...] = krope[:, :A_ROPE]
    kinv_ref[...] = kinv8


def _mla_prep(q_lat, kvr, cos, sin, consts, tm, sample):
    T = q_lat.shape[0]
    rows = [q_lat, kvr, cos, sin]
    in_specs = []
    for a in rows:
        per = a.shape[0] // tm
        if a.shape[0] == T:
            in_specs.append(pl.BlockSpec((tm, a.shape[1]), lambda i: (i, 0)))
        else:
            in_specs.append(pl.BlockSpec((tm, a.shape[1]), lambda i, per=per: (i % per, 0)))
    in_specs += [_const_spec(a) for a in consts]
    row_out = lambda c: pl.BlockSpec((tm, c), lambda i: (i, 0))
    tail_shapes = [jax.ShapeDtypeStruct((T, KV_LORA), F32), jax.ShapeDtypeStruct((T, A_ROPE), F32),
                   jax.ShapeDtypeStruct((T, A_HEADS), F32)]
    tail_specs = [row_out(KV_LORA), row_out(A_ROPE), row_out(A_HEADS)]
    if sample:
        kern = _mla_sample_kernel
        out_shape = [jax.ShapeDtypeStruct((T, A_HEADS * KV_LORA), BF16),
                     jax.ShapeDtypeStruct((T, A_HEADS * A_ROPE), BF16)] + tail_shapes
        out_specs = [row_out(A_HEADS * KV_LORA), row_out(A_HEADS * A_ROPE)] + tail_specs
    else:
        kern = _mla_prompt_kernel
        dqk = A_NOPE + A_ROPE
        head_spec = pl.BlockSpec((A_HEADS, tm, dqk), lambda i: (0, i, 0))
        out_shape = [jax.ShapeDtypeStruct((A_HEADS, T, dqk), BF16), jax.ShapeDtypeStruct((A_HEADS, T, dqk), BF16),
                     jax.ShapeDtypeStruct((T, A_HEADS * A_VDIM), BF16)] + tail_shapes
        out_specs = [head_spec, head_spec, row_out(A_HEADS * A_VDIM)] + tail_specs
    return pl.pallas_call(kern, grid=(T // tm,), in_specs=in_specs, out_specs=out_specs, out_shape=out_shape,
                          compiler_params=_params(("parallel",)),
                          name="mla_prep_sample" if sample else "mla_prep_prompt")(*rows, *consts)


def _flash_kernel(q_ref, k_ref, v_ref, o_ref, *, tq):
    i = pl.program_id(2)
    q = q_ref[0]
    row = lax.broadcasted_iota(jnp.int32, (tq, tq), 0)
    col = lax.broadcasted_iota(jnp.int32, (tq, tq), 1)

    def body(j, carry):
        m, l, acc = carry
        start = pl.multiple_of(j * tq, tq)
        k = k_ref[0, pl.ds(start, tq), :]
        v = v_ref[pl.ds(start, tq), :]
        s = lax.dot_general(q, k, (((1,), (1,)), ((), ())), preferred_element_type=F32) * ATT_SCALE
        s = jnp.where(j * tq + col <= i * tq + row, s, -jnp.inf)
        m_new = jnp.maximum(m, jnp.max(s, axis=-1, keepdims=True))
        alpha = jnp.exp(m - m_new)
        p = jnp.exp(s - m_new)
        l = alpha * l + jnp.sum(p, axis=-1, keepdims=True)
        acc = alpha * acc + jnp.dot(p.astype(BF16), v, preferred_element_type=F32)
        return m_new, l, acc

    init = (jnp.full((tq, 1), -jnp.inf, F32), jnp.zeros((tq, 1), F32), jnp.zeros((tq, A_VDIM), F32))
    _, l, acc = lax.fori_loop(0, i + 1, body, init)
    o_ref[...] = (acc / l).astype(o_ref.dtype)


def _flash_prompt(qcat, kcat, v, Bsz, S, tq):
    nq = S // tq
    dqk = A_NOPE + A_ROPE
    T = Bsz * S
    return pl.pallas_call(
        functools.partial(_flash_kernel, tq=tq),
        grid=(Bsz, A_HEADS, nq),
        in_specs=[pl.BlockSpec((1, tq, dqk), lambda b, h, i: (h, b * nq + i, 0)),
                  pl.BlockSpec((1, S, dqk), lambda b, h, i: (h, b, 0)),
                  pl.BlockSpec((S, A_VDIM), lambda b, h, i: (b, h))],
        out_specs=pl.BlockSpec((tq, A_VDIM), lambda b, h, i: (b * nq + i, h)),
        out_shape=jax.ShapeDtypeStruct((T, A_HEADS * A_VDIM), BF16),
        compiler_params=_params(("parallel", "parallel", "arbitrary")),
        name="flash_prompt")(qcat, kcat, v)


NEW_PAD = 16


def _decode_kernel(pt_ref, qa_ref, qr_ref, cn_ref, krn_ref, ksn_ref, *rest, npg):
    ck, kr, ks = rest[:npg], rest[npg:2 * npg], rest[2 * npg:3 * npg]
    o_ref = rest[3 * npg]
    m_scr, l_scr, acc_scr = rest[3 * npg + 1:]
    j = pl.program_id(1)
    rows = qa_ref.shape[1]
    reps = rows // A_HEADS

    @pl.when(j == 0)
    def _():
        m_scr[...] = jnp.full(m_scr.shape, -jnp.inf, F32)
        l_scr[...] = jnp.zeros(l_scr.shape, F32)
        acc_scr[...] = jnp.zeros(acc_scr.shape, F32)

    qa = qa_ref[0]
    qr = qr_ref[0]
    nt = (((1,), (1,)), ((), ()))
    ss, ccs = [], []
    for p in range(npg):
        cc = ck[p][0, 0].astype(BF16)
        s = lax.dot_general(qa, cc, nt, preferred_element_type=F32)
        s = s * jnp.tile(ks[p][0, 0], (reps, 1))
        s = s + lax.dot_general(qr, kr[p][0, 0].astype(BF16), nt, preferred_element_type=F32)
        ss.append(s * ATT_SCALE)
        ccs.append(cc)
    mstep = jnp.max(ss[0], axis=-1, keepdims=True)
    for s in ss[1:]:
        mstep = jnp.maximum(mstep, jnp.max(s, axis=-1, keepdims=True))
    m_old = m_scr[...]
    m_new = jnp.maximum(m_old, mstep)
    alpha = jnp.exp(m_old - m_new)
    l = alpha * l_scr[...]
    acc = alpha * acc_scr[...]
    for s, cc in zip(ss, ccs):
        e = jnp.exp(s - m_new)
        l = l + jnp.sum(e, axis=-1, keepdims=True)
        acc = acc + jnp.dot(e.astype(BF16), cc, preferred_element_type=F32)
    m_scr[...] = m_new
    l_scr[...] = l
    acc_scr[...] = acc

    @pl.when(j == pl.num_programs(1) - 1)
    def _():
        cn = cn_ref[0].astype(BF16)
        s = lax.dot_general(qa, cn, nt, preferred_element_type=F32) * ksn_ref[0]
        s = s + lax.dot_general(qr, krn_ref[0].astype(BF16), nt, preferred_element_type=F32)
        row = lax.broadcasted_iota(jnp.int32, s.shape, 0)
        col = lax.broadcasted_iota(jnp.int32, s.shape, 1)
        s = jnp.where(col <= (row >> 3), s * ATT_SCALE, -jnp.inf)
        m1 = m_scr[...]
        m2 = jnp.maximum(m1, jnp.max(s, axis=-1, keepdims=True))
        a2 = jnp.exp(m1 - m2)
        e = jnp.exp(s - m2)
        l2 = a2 * l_scr[...] + jnp.sum(e, axis=-1, keepdims=True)
        acc2 = a2 * acc_scr[...] + jnp.dot(e.astype(BF16), cn, preferred_element_type=F32)
        o_ref[0] = acc2 / l2


def _decode_attend(page_table, qa, qr, cn, krn, ksn, cache_ckv, cache_krope, cache_kscale_t, li, npg):
    Bd, n_pages = page_table.shape
    rows = qa.shape[1]
    assert n_pages % npg == 0
    per_b = lambda b, j, pt: (b, 0, 0)

    def page_map(p):
        return lambda b, j, pt: (li, pt[b, j * npg + p], 0, 0)

    in_specs = [pl.BlockSpec((1, rows, KV_LORA), per_b), pl.BlockSpec((1, rows, A_ROPE), per_b),
                pl.BlockSpec((1, NEW_PAD, KV_LORA), per_b), pl.BlockSpec((1, NEW_PAD, A_ROPE), per_b),
                pl.BlockSpec((1, rows, NEW_PAD), per_b)]
    in_specs += [pl.BlockSpec((1, 1, PAGE_SIZE, KV_LORA), page_map(p)) for p in range(npg)]
    in_specs += [pl.BlockSpec((1, 1, PAGE_SIZE, A_ROPE), page_map(p)) for p in range(npg)]
    in_specs += [pl.BlockSpec((1, 1, A_HEADS, PAGE_SIZE), page_map(p)) for p in range(npg)]
    grid_spec = pltpu.PrefetchScalarGridSpec(
        num_scalar_prefetch=1, grid=(Bd, n_pages // npg), in_specs=in_specs,
        out_specs=pl.BlockSpec((1, rows, KV_LORA), per_b),
        scratch_shapes=[pltpu.VMEM((rows, 1), F32), pltpu.VMEM((rows, 1), F32), pltpu.VMEM((rows, KV_LORA), F32)])
    return pl.pallas_call(
        functools.partial(_decode_kernel, npg=npg), grid_spec=grid_spec,
        out_shape=jax.ShapeDtypeStruct((Bd, rows, KV_LORA), F32),
        compiler_params=_params(("parallel", "arbitrary")), name="decode_attend")(
            page_table, qa, qr, cn, krn, ksn,
            *([cache_ckv] * npg), *([cache_krope] * npg), *([cache_kscale_t] * npg))


def _tri_masks(c):
    r = lax.broadcasted_iota(jnp.int32, (c, c), 0)
    col = lax.broadcasted_iota(jnp.int32, (c, c), 1)
    return r, col


def _conv_silu(x_ref, buf0_ref, bufout_ref, xe_ref, w_ref, bias, j, c):
    @pl.when(j == 0)
    def _():
        xe_ref[5:8, :] = buf0_ref[0]

    xe_ref[8:8 + c, :] = x_ref[0]
    y = xe_ref[pl.ds(5, c), :] * w_ref[0:1, :]
    for k in range(1, 4):
        y = y + xe_ref[pl.ds(5 + k, c), :] * w_ref[k:k + 1, :]
    if bias is not None:
        y = y + bias
    bufout_ref[0] = xe_ref[c + 5:c + 8, :]
    tail = xe_ref[c:c + 8, :]
    xe_ref[0:8, :] = tail
    return _silu(y)


def _ssd_kernel(xbc_ref, z_ref, small_ref, smallt_ref, buf0_ref, s0_ref, cw_ref, cb_ref, dtb_ref, alog_ref,
                dtbc_ref, alogc_ref, dskip_ref, ng_ref, y_ref, bufout_ref, sout_ref, xe_ref, *, c):
    j = pl.program_id(1)

    @pl.when(j == 0)
    def _():
        sout_ref[0] = s0_ref[0]

    xbc = _conv_silu(xbc_ref, buf0_ref, bufout_ref, xe_ref, cw_ref, cb_ref[...], j, c)
    xs = xbc[:, :M_WIDTH]
    gw = M_GROUPS * M_STATE
    bm = xbc[:, M_WIDTH:M_WIDTH + gw]
    cm = xbc[:, M_WIDTH + gw:]
    dt = _softplus(small_ref[0] + dtb_ref[...])
    a = dt * (-jnp.exp(alog_ref[...]))
    dtt = _softplus(smallt_ref[0, 0][:M_HEADS] + dtbc_ref[...])
    at = dtt * (-jnp.exp(alogc_ref[...]))
    r, col = _tri_masks(c)
    tri = col <= r
    cum = _hdot(tri.astype(F32), a)
    cumt = _hdot(at, (r <= col).astype(F32))
    rep = M_HEADS // M_GROUPS
    cb = [_mm_nt(cm[:, g * M_STATE:(g + 1) * M_STATE], bm[:, g * M_STATE:(g + 1) * M_STATE])
          for g in range(M_GROUPS)]
    ys = []
    for h in range(M_HEADS):
        g = h // rep
        cc = cum[:, h:h + 1]
        cr = cumt[h:h + 1, :]
        lmat = jnp.exp(jnp.where(tri, cc - cr, -jnp.inf))
        mmat = cb[g] * lmat * dtt[h:h + 1, :]
        xh = xs[:, h * M_HEADDIM:(h + 1) * M_HEADDIM]
        bg = bm[:, g * M_STATE:(g + 1) * M_STATE]
        cg = cm[:, g * M_STATE:(g + 1) * M_STATE]
        s_h = sout_ref[0, h]
        y = _mm(mmat, xh) + _mm_nt(cg, s_h) * jnp.exp(cc)
        clast = cum[c - 1:c, h:h + 1]
        to_end = jnp.exp(clast - cc) * dt[:, h:h + 1]
        sout_ref[0, h] = s_h * jnp.exp(clast) + _mm_tn(xh * to_end, bg)
        ys.append(y)
    y = jnp.concatenate(ys, axis=-1) + xs * dskip_ref[...]
    y = y * _silu(z_ref[0])
    y_ref[0] = _rms(y, ng_ref[...])


def _ssd_scan(xbc, z, small, smallt, buf0, s0, consts, c):
    Bsz, L, _ = xbc.shape
    nc = L // c
    blk = lambda w: pl.BlockSpec((1, c, w), lambda b, j: (b, j, 0))
    in_specs = [blk(M_CONV_CH), blk(M_WIDTH), blk(SMALL_W),
                pl.BlockSpec((1, 1, 16, c), lambda b, j: (b, j, 0, 0)),
                pl.BlockSpec((1, M_CONV - 1, M_CONV_CH), lambda b, j: (b, 0, 0)),
                pl.BlockSpec((1, M_HEADS, M_HEADDIM, M_STATE), lambda b, j: (b, 0, 0, 0))]
    in_specs += [_const_spec(a) for a in consts]
    out_specs = [blk(M_WIDTH),
                 pl.BlockSpec((1, M_CONV - 1, M_CONV_CH), lambda b, j: (b, 0, 0)),
                 pl.BlockSpec((1, M_HEADS, M_HEADDIM, M_STATE), lambda b, j: (b, 0, 0, 0))]
    out_shape = [jax.ShapeDtypeStruct((Bsz, L, M_WIDTH), F32),
                 jax.ShapeDtypeStruct((Bsz, M_CONV - 1, M_CONV_CH), F32),
                 jax.ShapeDtypeStruct((Bsz, M_HEADS, M_HEADDIM, M_STATE), F32)]
    return pl.pallas_call(
        functools.partial(_ssd_kernel, c=c), grid=(Bsz, nc), in_specs=in_specs, out_specs=out_specs,
        out_shape=out_shape, scratch_shapes=[pltpu.VMEM((c + 8, M_CONV_CH), F32)],
        compiler_params=_params(("parallel", "arbitrary")), name="ssd_scan")(
            xbc, z, small, smallt, buf0, s0, *consts)


def _unit_lower_inv(amat, c, r, col):
    eye = (r == col).astype(F32)
    a8 = jnp.where((r >> 3) == (col >> 3), amat, 0.0)
    x = eye - a8
    p = _hdot(a8, a8)
    x = x + _hdot(x, p)
    p = _hdot(p, p)
    x = x + _hdot(x, p)
    b = 8
    while b < c:
        sh = int(math.log2(b))
        low = jnp.where((r >> (sh + 1)) == (col >> (sh + 1)), jnp.where((r >> sh) != (col >> sh), amat, 0.0), 0.0)
        x = x - _hdot(_hdot(x, low), x)
        b *= 2
    return x


def _gdn_kernel(qkv_ref, z_ref, small_ref, smallt_ref, buf0_ref, s0_ref, cw_ref, gdt_ref, galog_ref,
                gdtc_ref, galogc_ref, ng_ref, o_ref, bufout_ref, sout_ref, xe_ref, *, c):
    j = pl.program_id(1)

    @pl.when(j == 0)
    def _():
        sout_ref[0] = s0_ref[0]

    qkv = _conv_silu(qkv_ref, buf0_ref, bufout_ref, xe_ref, cw_ref, None, j, c)
    nq = G_HEADS * G_DK
    small = small_ref[0]
    gall = -jnp.exp(galog_ref[...]) * _softplus(small + gdt_ref[...])
    gallt = -jnp.exp(galogc_ref[...]) * _softplus(smallt_ref[0, 0] + gdtc_ref[...])
    r, col = _tri_masks(c)
    incl = col <= r
    strict = col < r
    gcum = _hdot(incl.astype(F32), gall)
    gcumt = _hdot(gallt, (r <= col).astype(F32))
    z = z_ref[0]
    outs = []
    for h in range(G_HEADS):
        qh = qkv[:, h * G_DK:(h + 1) * G_DK]
        kh = qkv[:, nq + h * G_DK:nq + (h + 1) * G_DK]
        vh = qkv[:, 2 * nq + h * G_DV:2 * nq + (h + 1) * G_DV]
        qh = qh * lax.rsqrt(jnp.sum(qh * qh, axis=-1, keepdims=True) + EPS) * (G_DK ** -0.5)
        kh = kh * lax.rsqrt(jnp.sum(kh * kh, axis=-1, keepdims=True) + EPS)
        beta = jax.nn.sigmoid(small[:, G_B_COL + h:G_B_COL + h + 1])
        gc = gcum[:, G_A_COL + h:G_A_COL + h + 1]
        gr = gcumt[G_A_COL + h:G_A_COL + h + 1, :]
        decay = jnp.exp(jnp.where(incl, gc - gr, -jnp.inf))
        kb = kh * beta
        amat = jnp.where(strict, _mm_nt(kb, kh) * decay, 0.0)
        tmat = _unit_lower_inv(amat, c, r, col)
        u = _mm(tmat, vh * beta)
        w = _mm(tmat, kb * jnp.exp(gc))
        attn = jnp.where(incl, _mm_nt(qh, kh) * decay, 0.0)
        s_h = sout_ref[0, h]
        v_new = u - _mm(w, s_h)
        o = _mm(qh * jnp.exp(gc), s_h) + _mm(attn, v_new)
        glast = gcum[c - 1:c, G_A_COL + h:G_A_COL + h + 1]
        sout_ref[0, h] = s_h * jnp.exp(glast) + _mm_tn(kh * jnp.exp(glast - gc), v_new)
        outs.append(_rms(o, ng_ref[...]) * _silu(z[:, h * G_DV:(h + 1) * G_DV]))
    o_ref[0] = jnp.concatenate(outs, axis=-1)


def _gdn_scan(qkv, z, small, smallt, buf0, s0, consts, c):
    Bsz, L, _ = qkv.shape
    nc = L // c
    blk = lambda w: pl.BlockSpec((1, c, w), lambda b, j: (b, j, 0))
    in_specs = [blk(G_CONV_CH), blk(G_WIDTH), blk(SMALL_W),
                pl.BlockSpec((1, 1, 16, c), lambda b, j: (b, j, 0, 0)),
                pl.BlockSpec((1, G_CONV - 1, G_CONV_CH), lambda b, j: (b, 0, 0)),
                pl.BlockSpec((1, G_HEADS, G_DK, G_DV), lambda b, j: (b, 0, 0, 0))]
    in_specs += [_const_spec(a) for a in consts]
    out_specs = [blk(G_WIDTH),
                 pl.BlockSpec((1, G_CONV - 1, G_CONV_CH), lambda b, j: (b, 0, 0)),
                 pl.BlockSpec((1, G_HEADS, G_DK, G_DV), lambda b, j: (b, 0, 0, 0))]
    out_shape = [jax.ShapeDtypeStruct((Bsz, L, G_WIDTH), F32),
                 jax.ShapeDtypeStruct((Bsz, G_CONV - 1, G_CONV_CH), F32),
                 jax.ShapeDtypeStruct((Bsz, G_HEADS, G_DK, G_DV), F32)]
    return pl.pallas_call(
        functools.partial(_gdn_kernel, c=c), grid=(Bsz, nc), in_specs=in_specs, out_specs=out_specs,
        out_shape=out_shape, scratch_shapes=[pltpu.VMEM((c + 8, G_CONV_CH), F32)],
        compiler_params=_params(("parallel", "arbitrary")), name="gdn_scan")(
            qkv, z, small, smallt, buf0, s0, *consts)


def _out_proj_prompt_body(rows, consts, outs):
    a, m, g, x = rows
    wa, wm, wg = consts
    outs[0][...] = x[...] + _mm(a[...], wa[...]) + _mm(m[...], wm[...]) + _mm(g[...], wg[...])


def _out_proj_sample_body(rows, consts, outs):
    olat, m, g, x = rows
    wvb, wa, wm, wg = consts
    heads = [_mm(olat[:, h * KV_LORA:(h + 1) * KV_LORA], wvb[h]) for h in range(A_HEADS)]
    a = jnp.concatenate(heads, axis=-1)
    outs[0][...] = x[...] + _mm(a, wa[...]) + _mm(m[...], wm[...]) + _mm(g[...], wg[...])


def _top_values(s, k):
    out = []
    for it in range(k):
        m = jnp.max(s, axis=0, keepdims=True)
        out.append(m)
        if it + 1 < k:
            s = jnp.where(s == m, -jnp.inf, s)
    return out


def _peer_query_kernel(x_ref, ln2_ref, wqt_ref, keys_ref, h2_ref, s1_ref, e1_ref, s2_ref, e2_ref, tau_ref):
    h2 = _rms(x_ref[...], ln2_ref[...]).astype(BF16)
    h2_ref[...] = h2
    qt = lax.dot_general(wqt_ref[...], h2, (((1,), (1,)), ((), ())), preferred_element_type=F32)
    qt = qt.astype(BF16)
    half = P_DKEY // 2
    taus = []
    for h in range(P_HEADS):
        s1 = jnp.dot(keys_ref[2 * h], qt[(2 * h) * half:(2 * h + 1) * half], preferred_element_type=F32)
        s2 = jnp.dot(keys_ref[2 * h + 1], qt[(2 * h + 1) * half:(2 * h + 2) * half], preferred_element_type=F32)
        v1 = _top_values(s1, P_TOPK)
        v2 = _top_values(s2, P_TOPK)
        v2m = jnp.concatenate(v2, axis=0)
        cand = jnp.concatenate([v1[a] + v2m for a in range(P_TOPK)], axis=0)
        best = _top_values(cand, P_TOPK)
        zsum = jnp.ones_like(best[0])
        for b in best[1:]:
            zsum = zsum + jnp.exp(b - best[0])
        taus.append(best[P_TOPK - 1])
        s1_ref[h] = s1
        s2_ref[h] = s2
        e1_ref[h] = jnp.exp(s1 - v1[0])
        e2_ref[h] = jnp.exp(s2 - v2[0]) / zsum
    tau_ref[...] = jnp.concatenate(taus, axis=0)


def _peer_query(x, ln2, wqt, keys, tm):
    T, D = x.shape
    hk = pl.BlockSpec((P_HEADS, P_NKEYS, tm), lambda i: (0, 0, i))
    hk_shape = jax.ShapeDtypeStruct((P_HEADS, P_NKEYS, T), F32)
    return pl.pallas_call(
        _peer_query_kernel, grid=(T // tm,),
        in_specs=[pl.BlockSpec((tm, D), lambda i: (i, 0)), _const_spec(ln2), _const_spec(wqt), _const_spec(keys)],
        out_specs=[pl.BlockSpec((tm, D), lambda i: (i, 0)), hk, hk, hk, hk,
                   pl.BlockSpec((P_HEADS, tm), lambda i: (0, i))],
        out_shape=[jax.ShapeDtypeStruct((T, D), BF16), hk_shape, hk_shape, hk_shape, hk_shape,
                   jax.ShapeDtypeStruct((P_HEADS, T), F32)],
        compiler_params=_params(("parallel",)), name="peer_query")(x, ln2, wqt, keys)


def _gelu(x):
    return 0.5 * x * (1.0 + lax.erf(x * (2.0 ** -0.5)))


def _peer_expert_kernel(h2_ref, s1_ref, e1_ref, s2_ref, e2_ref, tau_ref, u_ref, vt_ref, x_ref, y_ref, acc_ref, *, ec):
    j = pl.program_id(1)

    @pl.when(j == 0)
    def _():
        acc_ref[...] = jnp.zeros(acc_ref.shape, F32)

    pt = lax.dot_general(u_ref[...], h2_ref[...], (((1,), (1,)), ((), ())), preferred_element_type=F32)
    nsub = ec // P_NKEYS
    acts = []
    for r in range(nsub):
        i1 = j * nsub + r
        g = None
        for h in range(P_HEADS):
            cand = s2_ref[h] + s1_ref[h, pl.ds(i1, 1), :]
            w = jnp.where(cand >= tau_ref[h:h + 1, :], e2_ref[h] * e1_ref[h, pl.ds(i1, 1), :], 0.0)
            g = w if g is None else g + w
        acts.append((_gelu(pt[r * P_NKEYS:(r + 1) * P_NKEYS]) * g).astype(BF16))
    act = jnp.concatenate(acts, axis=0)
    acc_ref[...] += jnp.dot(vt_ref[...], act, preferred_element_type=F32)

    @pl.when(j == pl.num_programs(1) - 1)
    def _():
        y_ref[...] = x_ref[...] + acc_ref[...].T


def _peer_experts(h2, s1, e1, s2, e2, tau, u, vt, x, tm, ec):
    T, D = x.shape
    ne = u.shape[0]
    hk = pl.BlockSpec((P_HEADS, P_NKEYS, tm), lambda i, j: (0, 0, i))
    tok = pl.BlockSpec((tm, D), lambda i, j: (i, 0))
    return pl.pallas_call(
        functools.partial(_peer_expert_kernel, ec=ec), grid=(T // tm, ne // ec),
        in_specs=[tok, hk, hk, hk, hk, pl.BlockSpec((P_HEADS, tm), lambda i, j: (0, i)),
                  pl.BlockSpec((ec, D), lambda i, j: (j, 0)), pl.BlockSpec((D, ec), lambda i, j: (0, j)), tok],
        out_specs=tok, out_shape=jax.ShapeDtypeStruct((T, D), F32),
        scratch_shapes=[pltpu.VMEM((D, tm), F32)],
        compiler_params=_params(("parallel", "arbitrary")), name="peer_experts")(
            h2, s1, e1, s2, e2, tau, u, vt, x)


def _row(v, width=None):
    v = v.astype(F32).reshape(1, -1)
    if width is not None and v.shape[1] < width:
        v = jnp.pad(v, ((0, 0), (0, width - v.shape[1])))
    return v


def _layer_weights(i, ln1, w_in, w_out, ln2, q_a_norm, w_qb, kv_a_norm, w_kb, w_vb, qn_nope, qn_rope, kn_nope,
                   kn_rope, m_conv_w, m_conv_b, m_dt_bias, m_A_log, m_D, m_norm, g_conv_w, g_dt_bias, g_A_log,
                   g_norm, p_wq, p_keys, p_u, p_v):
    D = w_in.shape[1]
    o = np.cumsum((0,) + IN_SIZES)
    wi = w_in[i]
    seg = lambda k: wi[:, o[k]:o[k + 1]]
    zeros = lambda n: jnp.zeros((D, n), F32)
    w_kvr = jnp.concatenate([seg(1), seg(2), zeros(LANES - A_ROPE)], axis=1)
    w_small = jnp.concatenate([seg(5), seg(8), seg(9), zeros(SMALL_W - M_HEADS - 2 * G_HEADS)], axis=1)
    in_ws = [w.astype(BF16) for w in (seg(0), w_kvr, seg(3), seg(4), seg(6), seg(7), w_small)]
    wqb = w_qb[i].reshape(Q_LORA, A_HEADS, A_NOPE + A_ROPE)
    w_qn = wqb[:, :, :A_NOPE].reshape(Q_LORA, A_HEADS * A_NOPE).astype(BF16)
    w_qr = wqb[:, :, A_NOPE:].reshape(Q_LORA, A_HEADS * A_ROPE).astype(BF16)
    wkb = w_kb[i]
    wvb = w_vb[i]
    mla_common = [_row(q_a_norm[i]), w_qn, w_qr, _row(qn_nope[i]), _row(jnp.tile(qn_rope[i], A_HEADS)),
                  _row(kv_a_norm[i]), _row(kn_rope[i], LANES), wkb.reshape(KV_LORA, A_HEADS * A_NOPE).astype(BF16),
                  _row(kn_nope[i])]
    mla_prompt = mla_common + [wvb.reshape(KV_LORA, A_HEADS * A_VDIM).astype(BF16)]
    mla_sample = mla_common + [jnp.transpose(wkb, (1, 2, 0)).astype(BF16)]
    wvb_heads = jnp.transpose(wvb, (1, 0, 2)).astype(BF16)
    col16 = lambda v, off: jnp.zeros((16, 1), F32).at[off:off + v.shape[0], 0].set(v.astype(F32))
    ssd = [m_conv_w[i].astype(F32), _row(m_conv_b[i]), _row(m_dt_bias[i], SMALL_W), _row(m_A_log[i], SMALL_W),
           m_dt_bias[i].astype(F32).reshape(M_HEADS, 1), m_A_log[i].astype(F32).reshape(M_HEADS, 1),
           _row(jnp.repeat(m_D[i], M_HEADDIM)), _row(m_norm[i])]
    pad_a = lambda v: jnp.zeros((1, SMALL_W), F32).at[0, G_A_COL:G_A_COL + G_HEADS].set(v.astype(F32))
    gdn = [g_conv_w[i].astype(F32), pad_a(g_dt_bias[i]), pad_a(g_A_log[i]),
           col16(g_dt_bias[i], G_A_COL), col16(g_A_log[i], G_A_COL), _row(g_norm[i])]
    wo = w_out[i].astype(BF16)
    aw = A_HEADS * A_VDIM
    out_ws = [wo[:aw], wo[aw:aw + M_WIDTH], wo[aw + M_WIDTH:]]
    peer = dict(ln2=_row(ln2[i]), wqt=p_wq[i].T.astype(BF16),
                keys=p_keys[i].reshape(2 * P_HEADS, P_NKEYS, P_DKEY // 2).astype(BF16),
                u=p_u[i].astype(BF16), vt=p_v[i].T.astype(BF16))
    return dict(ln1=_row(ln1[i]), in_ws=in_ws, mla_prompt=mla_prompt, mla_sample=mla_sample, wvb_heads=wvb_heads,
                ssd=ssd, gdn=gdn, out_ws=out_ws, peer=peer)


def _rope_tables(pos):
    half = A_ROPE // 2
    inv = ROPE_THETA ** (-jnp.arange(half, dtype=F32) / half)
    ang = pos.astype(F32)[:, None] * inv[None, :]
    cos, sin = jnp.cos(ang), jnp.sin(ang)
    reps = LANES // A_ROPE
    cos_t = jnp.tile(jnp.concatenate([cos, cos], axis=1), (1, reps))
    sin_t = jnp.tile(jnp.concatenate([-sin, sin], axis=1), (1, reps))
    return cos_t, sin_t


def _small_t(small, Bsz, L, c):
    s = small[:, :16].reshape(Bsz, L // c, c, 16)
    return jnp.swapaxes(s, 2, 3)


def _pick_tile(T, cap):
    t = cap
    while T % t:
        t //= 2
    return t


def _token_layer(x, Bsz, L, cos, sin, W, m_buf, m_s, g_buf, g_s, attend, sample):
    T = x.shape[0]
    tm = _pick_tile(T, 256)
    q_lat, kvr, m_z, m_xbc, g_qkv, g_z, small = _in_proj(x, W["ln1"], W["in_ws"], tm)
    mla = _mla_prep(q_lat, kvr, cos, sin, W["mla_sample"] if sample else W["mla_prompt"], tm, sample)
    a_out, cvec, krope, kinv = attend(mla)
    c = CHUNK if L % CHUNK == 0 else L
    smallt = _small_t(small, Bsz, L, c)
    r3 = lambda a: a.reshape(Bsz, L, a.shape[-1])
    m_y, m_buf, m_s = _ssd_scan(r3(m_xbc), r3(m_z), r3(small), smallt, m_buf, m_s, W["ssd"], c)
    g_y, g_buf, g_s = _gdn_scan(r3(g_qkv), r3(g_z), r3(small), smallt, g_buf, g_s, W["gdn"], c)
    m_y = m_y.reshape(T, M_WIDTH)
    g_y = g_y.reshape(T, G_WIDTH)
    D = x.shape[1]
    if sample:
        (x,) = _rowwise(_out_proj_sample_body, [a_out, m_y, g_y, x], [W["wvb_heads"]] + W["out_ws"],
                        [(D, F32)], tm, "out_proj_sample")
    else:
        (x,) = _rowwise(_out_proj_prompt_body, [a_out, m_y, g_y, x], W["out_ws"], [(D, F32)], tm, "out_proj_prompt")
    pw = W["peer"]
    h2, s1, e1, s2, e2, tau = _peer_query(x, pw["ln2"], pw["wqt"], pw["keys"], tm)
    x = _peer_experts(h2, s1, e1, s2, e2, tau, pw["u"], pw["vt"], x, tm, 512)
    return x, (cvec, krope, kinv, m_s, m_buf, g_s, g_buf)


def kernel(x_prompt, x_sample, cache_ckv, cache_krope, cache_kscale, state_ssm, state_ssm_conv, state_gdn, state_gdn_conv, page_table, ln1, w_in, w_out, ln2, q_a_norm, w_qb, kv_a_norm, w_kb, w_vb, qn_nope, qn_rope, kn_nope, kn_rope, m_conv_w, m_conv_b, m_dt_bias, m_A_log, m_D, m_norm, g_conv_w, g_dt_bias, g_A_log, g_norm, p_wq, p_keys, p_u, p_v):
    B, S, D = x_prompt.shape
    Bd, Ld, _ = x_sample.shape
    depth = ln1.shape[0]
    n_pages = page_table.shape[1]
    past = n_pages * PAGE_SIZE
    cos_p, sin_p = _rope_tables(jnp.arange(S))
    cos_s, sin_s = _rope_tables(past + jnp.arange(Ld))
    cos_s, sin_s = jnp.tile(cos_s, (Bd, 1)), jnp.tile(sin_s, (Bd, 1))
    kscale_t = jnp.swapaxes(cache_kscale, -1, -2)
    npg = _pick_tile(n_pages, 16)
    tq = _pick_tile(S, 256)
    xp = x_prompt.reshape(B * S, D)
    xs = x_sample.reshape(Bd * Ld, D)
    new_p = [[] for _ in range(7)]
    new_s = [[] for _ in range(7)]
    weights = (ln1, w_in, w_out, ln2, q_a_norm, w_qb, kv_a_norm, w_kb, w_vb, qn_nope, qn_rope, kn_nope, kn_rope,
               m_conv_w, m_conv_b, m_dt_bias, m_A_log, m_D, m_norm, g_conv_w, g_dt_bias, g_A_log, g_norm,
               p_wq, p_keys, p_u, p_v)
    for i in range(depth):
        W = _layer_weights(i, *weights)

        def attend_prompt(mla):
            qcat, kcat, v, cvec, krope, kinv = mla
            return _flash_prompt(qcat, kcat, v, B, S, tq), cvec, krope, kinv

        def attend_sample(mla, i=i):
            qa, qr, cvec, krope, kinv = mla
            rows = Ld * A_HEADS
            padk = lambda a: jnp.pad(a.reshape(Bd, Ld, a.shape[-1]), ((0, 0), (0, NEW_PAD - Ld), (0, 0)))
            ksn = jnp.swapaxes(kinv.reshape(Bd, Ld, A_HEADS), 1, 2)
            ksn = jnp.pad(jnp.tile(ksn, (1, Ld, 1)), ((0, 0), (0, 0), (0, NEW_PAD - Ld)))
            o_lat = _decode_attend(page_table, qa.reshape(Bd, rows, KV_LORA), qr.reshape(Bd, rows, A_ROPE),
                                   padk(cvec), padk(krope), ksn, cache_ckv, cache_krope, kscale_t, i, npg)
            return o_lat.reshape(Bd * Ld, A_HEADS * KV_LORA), cvec, krope, kinv

        zeros = lambda *s: jnp.zeros(s, F32)
        xp, st_p = _token_layer(xp, B, S, cos_p, sin_p, W, zeros(B, M_CONV - 1, M_CONV_CH),
                                zeros(B, M_HEADS, M_HEADDIM, M_STATE), zeros(B, G_CONV - 1, G_CONV_CH),
                                zeros(B, G_HEADS, G_DK, G_DV), attend_prompt, False)
        xs, st_s = _token_layer(xs, Bd, Ld, cos_s, sin_s, W, state_ssm_conv[i], state_ssm[i],
                                state_gdn_conv[i], state_gdn[i], attend_sample, True)
        for lst, val in zip(new_p, st_p):
            lst.append(val)
        for lst, val in zip(new_s, st_s):
            lst.append(val)

    def pack(vals, Bsz, L):
        cvec, krope, kinv, m_s, m_buf, g_s, g_buf = [jnp.stack(v) for v in vals]
        r = lambda a: a.reshape(depth, Bsz, L, a.shape[-1])
        return r(cvec), r(krope), r(kinv), m_s, m_buf, g_s, g_buf

    return (xp.reshape(B, S, D), xs.reshape(Bd, Ld, D)) + pack(new_p, B, S) + pack(new_s, Bd, Ld)
```

```python
import functools
import math

import jax
import jax.numpy as jnp
import numpy as np
from jax import lax
from jax.experimental import pallas as pl
from jax.experimental.pallas import tpu as pltpu

F32 = jnp.float32
BF16 = jnp.bfloat16
EPS = 1e-6

A_HEADS, A_NOPE, A_ROPE, A_VDIM = 8, 128, 64, 128
Q_LORA, KV_LORA = 384, 256
ROPE_THETA = 10000.0
M_HEADS, M_HEADDIM, M_WIDTH, M_GROUPS, M_STATE, M_CONV = 8, 64, 512, 2, 128, 4
M_CONV_CH = M_WIDTH + 2 * M_GROUPS * M_STATE
G_HEADS, G_DK, G_DV, G_WIDTH, G_CONV = 4, 128, 128, 512, 4
G_CONV_CH = 2 * G_HEADS * G_DK + G_WIDTH
IN_SIZES = (Q_LORA, KV_LORA, A_ROPE, M_WIDTH, M_CONV_CH, M_HEADS, G_CONV_CH, G_WIDTH, G_HEADS, G_HEADS)
P_HEADS, P_NKEYS, P_DKEY, P_TOPK = 8, 128, 256, 16
PAGE_SIZE = 128
CHUNK = 64
LANES = 128
SMALL_W = LANES
G_B_COL, G_A_COL = M_HEADS, M_HEADS + G_HEADS
VMEM_LIMIT = 56 * 1024 * 1024
ATT_SCALE = (A_NOPE + A_ROPE) ** -0.5


def _mm(a, b):
    return jnp.dot(a.astype(BF16), b.astype(BF16), preferred_element_type=F32)


def _mm_nt(a, b):
    return lax.dot_general(a.astype(BF16), b.astype(BF16), (((1,), (1,)), ((), ())),
                           preferred_element_type=F32)


def _mm_tn(a, b):
    return lax.dot_general(a.astype(BF16), b.astype(BF16), (((0,), (0,)), ((), ())),
                           preferred_element_type=F32)


def _hdot(a, b):
    return jnp.dot(a, b, precision=lax.Precision.HIGHEST, preferred_element_type=F32)


def _rms(x, g):
    return x * lax.rsqrt(jnp.mean(x * x, axis=-1, keepdims=True) + EPS) * g


def _softplus(x):
    return jnp.maximum(x, 0.0) + jnp.log1p(jnp.exp(-jnp.abs(x)))


def _silu(x):
    return x * jax.nn.sigmoid(x)


def _const_spec(a):
    nd = a.ndim
    return pl.BlockSpec(a.shape, lambda *_: (0,) * nd, pipeline_mode=pl.Buffered(1))


def _params(sem):
    return pltpu.CompilerParams(dimension_semantics=sem, vmem_limit_bytes=VMEM_LIMIT)


def _rowwise(body, rows, consts, outs, tm, name):
    T = rows[0].shape[0]
    assert T % tm == 0
    nr, nc = len(rows), len(consts)

    def kern(*refs):
        body(refs[:nr], refs[nr:nr + nc], refs[nr + nc:])

    in_specs = []
    for a in rows:
        assert a.shape[0] % tm == 0
        per = a.shape[0] // tm
        if a.shape[0] == T:
            in_specs.append(pl.BlockSpec((tm, a.shape[1]), lambda i: (i, 0)))
        else:
            in_specs.append(pl.BlockSpec((tm, a.shape[1]), lambda i, per=per: (i % per, 0)))
    in_specs += [_const_spec(a) for a in consts]
    out_specs = [pl.BlockSpec((tm, c), lambda i: (i, 0)) for c, _ in outs]
    out_shape = [jax.ShapeDtypeStruct((T, c), dt) for c, dt in outs]
    return pl.pallas_call(kern, grid=(T // tm,), in_specs=in_specs, out_specs=out_specs,
                          out_shape=out_shape, compiler_params=_params(("parallel",)),
                          name=name)(*rows, *consts)


def _in_proj_body(rows, consts, outs):
    x = rows[0][...]
    h = _rms(x, consts[0][...]).astype(BF16)
    for w_ref, o_ref in zip(consts[1:], outs):
        o_ref[...] = jnp.dot(h, w_ref[...], preferred_element_type=F32)


def _in_proj(x, ln1, ws, tm):
    outs = [(w.shape[1], F32) for w in ws]
    return _rowwise(_in_proj_body, [x], [ln1] + list(ws), outs, tm, "in_proj")


def _rope_rot(x, cos, sin_signed):
    w = x.shape[-1]
    lane = lax.broadcasted_iota(jnp.int32, x.shape, 1)
    first = (lane & (A_ROPE - 1)) < (A_ROPE // 2)
    rolled = jnp.where(first, pltpu.roll(x, w - A_ROPE // 2, 1), pltpu.roll(x, A_ROPE // 2, 1))
    return x * cos + rolled * sin_signed


def _mla_common(q_lat, kvr, cos, sin, qan, w_qn, w_qr, qn_rope, kvan, kn_rope, w_kb):
    ql = _rms(q_lat, qan).astype(BF16)
    qn = jnp.dot(ql, w_qn, preferred_element_type=F32)
    qr = jnp.dot(ql, w_qr, preferred_element_type=F32)
    wq = qr.shape[1]
    r = lax.broadcasted_iota(jnp.int32, (wq, wq), 0) >> 6
    c = lax.broadcasted_iota(jnp.int32, (wq, wq), 1) >> 6
    head_ones = (r == c).astype(F32)
    ss = _hdot(qr * qr, head_ones)
    qr = qr * lax.rsqrt(ss * (1.0 / A_ROPE) + EPS) * qn_rope
    reps = wq // LANES
    qr = _rope_rot(qr, jnp.tile(cos, (1, reps)), jnp.tile(sin, (1, reps)))
    kv = kvr[:, :KV_LORA]
    kr = kvr[:, KV_LORA:]
    cvec = _rms(kv, kvan)
    krn = kr * lax.rsqrt(jnp.sum(kr * kr, axis=-1, keepdims=True) * (1.0 / A_ROPE) + EPS) * kn_rope
    krope = _rope_rot(krn, cos, sin)
    kraw = jnp.dot(cvec.astype(BF16), w_kb, preferred_element_type=F32)
    return qn, qr, cvec, krope, kraw


def _k_inv_heads(kraw):
    tm = kraw.shape[0]
    lane8 = lax.broadcasted_iota(jnp.int32, (tm, A_HEADS), 1)
    kinv8 = jnp.zeros((tm, A_HEADS), F32)
    invs = []
    for h in range(A_HEADS):
        blk = kraw[:, h * A_NOPE:(h + 1) * A_NOPE]
        inv = lax.rsqrt(jnp.mean(blk * blk, axis=-1, keepdims=True) + EPS)
        invs.append(inv)
        kinv8 = jnp.where(lane8 == h, inv, kinv8)
    return invs, kinv8


def _mla_prompt_kernel(qlat_ref, kvr_ref, cos_ref, sin_ref, qan_ref, wqn_ref, wqr_ref, qnn_ref, qnr_ref,
                       kvan_ref, knr_ref, wkb_ref, knn_ref, wvb_ref,
                       qcat_ref, kcat_ref, v_ref, c_ref, krope_ref, kinv_ref):
    qn, qr, cvec, krope, kraw = _mla_common(
        qlat_ref[...], kvr_ref[...], cos_ref[...], sin_ref[...], qan_ref[...], wqn_ref[...], wqr_ref[...],
        qnr_ref[...], kvan_ref[...], knr_ref[...], wkb_ref[...])
    invs, kinv8 = _k_inv_heads(kraw)
    kr64 = krope[:, :A_ROPE]
    for h in range(A_HEADS):
        qn_h = _rms(qn[:, h * A_NOPE:(h + 1) * A_NOPE], qnn_ref[...])
        qcat_ref[h] = jnp.concatenate([qn_h, qr[:, h * A_ROPE:(h + 1) * A_ROPE]], axis=-1).astype(BF16)
        kn_h = kraw[:, h * A_NOPE:(h + 1) * A_NOPE] * invs[h] * knn_ref[...]
        kcat_ref[h] = jnp.concatenate([kn_h, kr64], axis=-1).astype(BF16)
    v_ref[...] = jnp.dot(cvec.astype(BF16), wvb_ref[...], preferred_element_type=F32).astype(BF16)
    c_ref[...] = cvec
    krope_ref[...] = kr64
    kinv_ref[...] = kinv8


def _mla_sample_kernel(qlat_ref, kvr_ref, cos_ref, sin_ref, qan_ref, wqn_ref, wqr_ref, qnn_ref, qnr_ref,
                       kvan_ref, knr_ref, wkb_ref, knn_ref, wkbt_ref,
                       qa_ref, qr_ref, c_ref, krope_ref, kinv_ref):
    qn, qr, cvec, krope, kraw = _mla_common(
        qlat_ref[...], kvr_ref[...], cos_ref[...], sin_ref[...], qan_ref[...], wqn_ref[...], wqr_ref[...],
        qnr_ref[...], kvan_ref[...], knr_ref[...], wkb_ref[...])
    _, kinv8 = _k_inv_heads(kraw)
    for h in range(A_HEADS):
        qn_h = _rms(qn[:, h * A_NOPE:(h + 1) * A_NOPE], qnn_ref[...]) * knn_ref[...]
        qa_ref[:, h * KV_LORA:(h + 1) * KV_LORA] = jnp.dot(
            qn_h.astype(BF16), wkbt_ref[h], preferred_element_type=F32).astype(BF16)
    qr_ref[...] = qr.astype(BF16)
    c_ref[...] = cvec
    krope_ref[...] = krope[:, :A_ROPE]
    kinv_ref[...] = kinv8


def _mla_prep(q_lat, kvr, cos, sin, consts, tm, sample):
    T = q_lat.shape[0]
    rows = [q_lat, kvr, cos, sin]
    in_specs = []
    for a in rows:
        per = a.shape[0] // tm
        if a.shape[0] == T:
            in_specs.append(pl.BlockSpec((tm, a.shape[1]), lambda i: (i, 0)))
        else:
            in_specs.append(pl.BlockSpec((tm, a.shape[1]), lambda i, per=per: (i % per, 0)))
    in_specs += [_const_spec(a) for a in consts]
    row_out = lambda c: pl.BlockSpec((tm, c), lambda i: (i, 0))
    tail_shapes = [jax.ShapeDtypeStruct((T, KV_LORA), F32), jax.ShapeDtypeStruct((T, A_ROPE), F32),
                   jax.ShapeDtypeStruct((T, A_HEADS), F32)]
    tail_specs = [row_out(KV_LORA), row_out(A_ROPE), row_out(A_HEADS)]
    if sample:
        kern = _mla_sample_kernel
        out_shape = [jax.ShapeDtypeStruct((T, A_HEADS * KV_LORA), BF16),
                     jax.ShapeDtypeStruct((T, A_HEADS * A_ROPE), BF16)] + tail_shapes
        out_specs = [row_out(A_HEADS * KV_LORA), row_out(A_HEADS * A_ROPE)] + tail_specs
    else:
        kern = _mla_prompt_kernel
        dqk = A_NOPE + A_ROPE
        head_spec = pl.BlockSpec((A_HEADS, tm, dqk), lambda i: (0, i, 0))
        out_shape = [jax.ShapeDtypeStruct((A_HEADS, T, dqk), BF16), jax.ShapeDtypeStruct((A_HEADS, T, dqk), BF16),
                     jax.ShapeDtypeStruct((T, A_HEADS * A_VDIM), BF16)] + tail_shapes
        out_specs = [head_spec, head_spec, row_out(A_HEADS * A_VDIM)] + tail_specs
    return pl.pallas_call(kern, grid=(T // tm,), in_specs=in_specs, out_specs=out_specs, out_shape=out_shape,
                          compiler_params=_params(("parallel",)),
                          name="mla_prep_sample" if sample else "mla_prep_prompt")(*rows, *consts)


def _flash_kernel(q_ref, k_ref, v_ref, o_ref, *, tq):
    i = pl.program_id(2)
    q = q_ref[0]
    row = lax.broadcasted_iota(jnp.int32, (tq, tq), 0)
    col = lax.broadcasted_iota(jnp.int32, (tq, tq), 1)

    def body(j, carry):
        m, l, acc = carry
        start = pl.multiple_of(j * tq, tq)
        k = k_ref[0, pl.ds(start, tq), :]
        v = v_ref[pl.ds(start, tq), :]
        s = lax.dot_general(q, k, (((1,), (1,)), ((), ())), preferred_element_type=F32) * ATT_SCALE
        s = jnp.where(j * tq + col <= i * tq + row, s, -jnp.inf)
        m_new = jnp.maximum(m, jnp.max(s, axis=-1, keepdims=True))
        alpha = jnp.exp(m - m_new)
        p = jnp.exp(s - m_new)
        l = alpha * l + jnp.sum(p, axis=-1, keepdims=True)
        acc = alpha * acc + jnp.dot(p.astype(BF16), v, preferred_element_type=F32)
        return m_new, l, acc

    init = (jnp.full((tq, 1), -jnp.inf, F32), jnp.zeros((tq, 1), F32), jnp.zeros((tq, A_VDIM), F32))
    _, l, acc = lax.fori_loop(0, i + 1, body, init)
    o_ref[...] = (acc / l).astype(o_ref.dtype)


def _flash_prompt(qcat, kcat, v, Bsz, S, tq):
    nq = S // tq
    dqk = A_NOPE + A_ROPE
    T = Bsz * S
    return pl.pallas_call(
        functools.partial(_flash_kernel, tq=tq),
        grid=(Bsz, A_HEADS, nq),
        in_specs=[pl.BlockSpec((1, tq, dqk), lambda b, h, i: (h, b * nq + i, 0)),
                  pl.BlockSpec((1, S, dqk), lambda b, h, i: (h, b, 0)),
                  pl.BlockSpec((S, A_VDIM), lambda b, h, i: (b, h))],
        out_specs=pl.BlockSpec((tq, A_VDIM), lambda b, h, i: (b * nq + i, h)),
        out_shape=jax.ShapeDtypeStruct((T, A_HEADS * A_VDIM), BF16),
        compiler_params=_params(("parallel", "parallel", "arbitrary")),
        name="flash_prompt")(qcat, kcat, v)


NEW_PAD = 16


def _decode_kernel(pt_ref, qa_ref, qr_ref, cn_ref, krn_ref, ksn_ref, *rest, npg):
    ck, kr, ks = rest[:npg], rest[npg:2 * npg], rest[2 * npg:3 * npg]
    o_ref = rest[3 * npg]
    m_scr, l_scr, acc_scr = rest[3 * npg + 1:]
    j = pl.program_id(1)
    rows = qa_ref.shape[1]
    reps = rows // A_HEADS

    @pl.when(j == 0)
    def _():
        m_scr[...] = jnp.full(m_scr.shape, -jnp.inf, F32)
        l_scr[...] = jnp.zeros(l_scr.shape, F32)
        acc_scr[...] = jnp.zeros(acc_scr.shape, F32)

    qa = qa_ref[0]
    qr = qr_ref[0]
    nt = (((1,), (1,)), ((), ()))
    ss, ccs = [], []
    for p in range(npg):
        cc = ck[p][0, 0].astype(BF16)
        s = lax.dot_general(qa, cc, nt, preferred_element_type=F32)
        s = s * jnp.tile(ks[p][0, 0].T, (reps, 1))
        s = s + lax.dot_general(qr, kr[p][0, 0].astype(BF16), nt, preferred_element_type=F32)
        ss.append(s * ATT_SCALE)
        ccs.append(cc)
    mstep = jnp.max(ss[0], axis=-1, keepdims=True)
    for s in ss[1:]:
        mstep = jnp.maximum(mstep, jnp.max(s, axis=-1, keepdims=True))
    m_old = m_scr[...]
    m_new = jnp.maximum(m_old, mstep)
    alpha = jnp.exp(m_old - m_new)
    l = alpha * l_scr[...]
    acc = alpha * acc_scr[...]
    for s, cc in zip(ss, ccs):
        e = jnp.exp(s - m_new)
        l = l + jnp.sum(e, axis=-1, keepdims=True)
        acc = acc + jnp.dot(e.astype(BF16), cc, preferred_element_type=F32)
    m_scr[...] = m_new
    l_scr[...] = l
    acc_scr[...] = acc

    @pl.when(j == pl.num_programs(1) - 1)
    def _():
        cn = cn_ref[0].astype(BF16)
        s = lax.dot_general(qa, cn, nt, preferred_element_type=F32) * ksn_ref[0]
        s = s + lax.dot_general(qr, krn_ref[0].astype(BF16), nt, preferred_element_type=F32)
        row = lax.broadcasted_iota(jnp.int32, s.shape, 0)
        col = lax.broadcasted_iota(jnp.int32, s.shape, 1)
        s = jnp.where(col <= (row >> 3), s * ATT_SCALE, -jnp.inf)
        m1 = m_scr[...]
        m2 = jnp.maximum(m1, jnp.max(s, axis=-1, keepdims=True))
        a2 = jnp.exp(m1 - m2)
        e = jnp.exp(s - m2)
        l2 = a2 * l_scr[...] + jnp.sum(e, axis=-1, keepdims=True)
        acc2 = a2 * acc_scr[...] + jnp.dot(e.astype(BF16), cn, preferred_element_type=F32)
        o_ref[0] = acc2 / l2


def _decode_attend(page_table, qa, qr, cn, krn, ksn, cache_ckv, cache_krope, cache_kscale, li, npg):
    Bd, n_pages = page_table.shape
    rows = qa.shape[1]
    assert n_pages % npg == 0
    per_b = lambda b, j, pt: (b, 0, 0)

    def page_map(p):
        return lambda b, j, pt: (li, pt[b, j * npg + p], 0, 0)

    in_specs = [pl.BlockSpec((1, rows, KV_LORA), per_b), pl.BlockSpec((1, rows, A_ROPE), per_b),
                pl.BlockSpec((1, NEW_PAD, KV_LORA), per_b), pl.BlockSpec((1, NEW_PAD, A_ROPE), per_b),
                pl.BlockSpec((1, rows, NEW_PAD), per_b)]
    in_specs += [pl.BlockSpec((1, 1, PAGE_SIZE, KV_LORA), page_map(p)) for p in range(npg)]
    in_specs += [pl.BlockSpec((1, 1, PAGE_SIZE, A_ROPE), page_map(p)) for p in range(npg)]
    in_specs += [pl.BlockSpec((1, 1, PAGE_SIZE, A_HEADS), page_map(p)) for p in range(npg)]
    grid_spec = pltpu.PrefetchScalarGridSpec(
        num_scalar_prefetch=1, grid=(Bd, n_pages // npg), in_specs=in_specs,
        out_specs=pl.BlockSpec((1, rows, KV_LORA), per_b),
        scratch_shapes=[pltpu.VMEM((rows, 1), F32), pltpu.VMEM((rows, 1), F32), pltpu.VMEM((rows, KV_LORA), F32)])
    return pl.pallas_call(
        functools.partial(_decode_kernel, npg=npg), grid_spec=grid_spec,
        out_shape=jax.ShapeDtypeStruct((Bd, rows, KV_LORA), F32),
        compiler_params=_params(("parallel", "arbitrary")), name="decode_attend")(
            page_table, qa, qr, cn, krn, ksn,
            *([cache_ckv] * npg), *([cache_krope] * npg), *([cache_kscale] * npg))


def _tri_masks(c):
    r = lax.broadcasted_iota(jnp.int32, (c, c), 0)
    col = lax.broadcasted_iota(jnp.int32, (c, c), 1)
    return r, col


def _conv_silu(x_ref, buf0_ref, bufout_ref, xe_ref, w_ref, bias, j, c, b=0):
    @pl.when(j == 0)
    def _():
        xe_ref[b, 5:8, :] = buf0_ref[b]

    xe_ref[b, 8:8 + c, :] = x_ref[b]
    y = xe_ref[b, pl.ds(5, c), :] * w_ref[0:1, :]
    for k in range(1, 4):
        y = y + xe_ref[b, pl.ds(5 + k, c), :] * w_ref[k:k + 1, :]
    if bias is not None:
        y = y + bias
    bufout_ref[b] = xe_ref[b, c + 5:c + 8, :]
    tail = xe_ref[b, c:c + 8, :]
    xe_ref[b, 0:8, :] = tail
    return _silu(y)


def _ssd_kernel(xbc_ref, z_ref, small_ref, smallt_ref, buf0_ref, s0_ref, cw_ref, cb_ref, dtb_ref, alog_ref,
                dtbc_ref, alogc_ref, dskip_ref, ng_ref, y_ref, bufout_ref, sout_ref, xe_ref, *, c):
    j = pl.program_id(1)

    @pl.when(j == 0)
    def _():
        sout_ref[0] = s0_ref[0]

    xbc = _conv_silu(xbc_ref, buf0_ref, bufout_ref, xe_ref, cw_ref, cb_ref[...], j, c)
    xs = xbc[:, :M_WIDTH]
    gw = M_GROUPS * M_STATE
    bm = xbc[:, M_WIDTH:M_WIDTH + gw]
    cm = xbc[:, M_WIDTH + gw:]
    dt = _softplus(small_ref[0] + dtb_ref[...])
    a = dt * (-jnp.exp(alog_ref[...]))
    dtt = _softplus(smallt_ref[0, 0][:M_HEADS] + dtbc_ref[...])
    at = dtt * (-jnp.exp(alogc_ref[...]))
    r, col = _tri_masks(c)
    tri = col <= r
    cum = _hdot(tri.astype(F32), a)
    cumt = _hdot(at, (r <= col).astype(F32))
    rep = M_HEADS // M_GROUPS
    cb = [_mm_nt(cm[:, g * M_STATE:(g + 1) * M_STATE], bm[:, g * M_STATE:(g + 1) * M_STATE])
          for g in range(M_GROUPS)]
    ys = []
    for h in range(M_HEADS):
        g = h // rep
        cc = cum[:, h:h + 1]
        cr = cumt[h:h + 1, :]
        lmat = jnp.exp(jnp.where(tri, cc - cr, -jnp.inf))
        mmat = cb[g] * lmat * dtt[h:h + 1, :]
        xh = xs[:, h * M_HEADDIM:(h + 1) * M_HEADDIM]
        bg = bm[:, g * M_STATE:(g + 1) * M_STATE]
        cg = cm[:, g * M_STATE:(g + 1) * M_STATE]
        s_h = sout_ref[0, h]
        y = _mm(mmat, xh) + _mm_nt(cg, s_h) * jnp.exp(cc)
        clast = cum[c - 1:c, h:h + 1]
        to_end = jnp.exp(clast - cc) * dt[:, h:h + 1]
        sout_ref[0, h] = s_h * jnp.exp(clast) + _mm_tn(xh * to_end, bg)
        ys.append(y)
    y = jnp.concatenate(ys, axis=-1) + xs * dskip_ref[...]
    y = y * _silu(z_ref[0])
    y_ref[0] = _rms(y, ng_ref[...])


def _ssd_scan(xbc, z, small, smallt, buf0, s0, consts, c):
    Bsz, L, _ = xbc.shape
    nc = L // c
    blk = lambda w: pl.BlockSpec((1, c, w), lambda b, j: (b, j, 0))
    in_specs = [blk(M_CONV_CH), blk(M_WIDTH), blk(SMALL_W),
                pl.BlockSpec((1, 1, 16, c), lambda b, j: (b, j, 0, 0)),
                pl.BlockSpec((1, M_CONV - 1, M_CONV_CH), lambda b, j: (b, 0, 0)),
                pl.BlockSpec((1, M_HEADS, M_HEADDIM, M_STATE), lambda b, j: (b, 0, 0, 0))]
    in_specs += [_const_spec(a) for a in consts]
    out_specs = [blk(M_WIDTH),
                 pl.BlockSpec((1, M_CONV - 1, M_CONV_CH), lambda b, j: (b, 0, 0)),
                 pl.BlockSpec((1, M_HEADS, M_HEADDIM, M_STATE), lambda b, j: (b, 0, 0, 0))]
    out_shape = [jax.ShapeDtypeStruct((Bsz, L, M_WIDTH), F32),
                 jax.ShapeDtypeStruct((Bsz, M_CONV - 1, M_CONV_CH), F32),
                 jax.ShapeDtypeStruct((Bsz, M_HEADS, M_HEADDIM, M_STATE), F32)]
    return pl.pallas_call(
        functools.partial(_ssd_kernel, c=c), grid=(Bsz, nc), in_specs=in_specs, out_specs=out_specs,
        out_shape=out_shape, scratch_shapes=[pltpu.VMEM((1, c + 8, M_CONV_CH), F32)],
        compiler_params=_params(("parallel", "arbitrary")), name="ssd_scan")(
            xbc, z, small, smallt, buf0, s0, *consts)


def _unit_lower_inv_many(amats, c, r, col):
    eye = (r == col).astype(F32)
    same8 = (r >> 3) == (col >> 3)
    a8 = [jnp.where(same8, a, 0.0) for a in amats]
    x = [eye - a for a in a8]
    p = [_hdot(a, a) for a in a8]
    x = [xi + _hdot(xi, pi) for xi, pi in zip(x, p)]
    p = [_hdot(pi, pi) for pi in p]
    x = [xi + _hdot(xi, pi) for xi, pi in zip(x, p)]
    b = 8
    while b < c:
        sh = int(math.log2(b))
        pick = ((r >> (sh + 1)) == (col >> (sh + 1))) & ((r >> sh) != (col >> sh))
        low = [jnp.where(pick, a, 0.0) for a in amats]
        t = [_hdot(xi, li) for xi, li in zip(x, low)]
        x = [xi - _hdot(ti, xi) for xi, ti in zip(x, t)]
        b *= 2
    return x


def _gdn_kernel(qkv_ref, z_ref, small_ref, smallt_ref, buf0_ref, s0_ref, cw_ref, gdt_ref, galog_ref,
                gdtc_ref, galogc_ref, ng_ref, o_ref, bufout_ref, sout_ref, xe_ref, *, c, nb):
    j = pl.program_id(1)

    @pl.when(j == 0)
    def _():
        sout_ref[...] = s0_ref[...]

    nq = G_HEADS * G_DK
    r, col = _tri_masks(c)
    incl = col <= r
    strict = col < r
    lower = incl.astype(F32)
    upper = (r <= col).astype(F32)
    chains = [(b, h) for b in range(nb) for h in range(G_HEADS)]
    qkv, small, gcum, gcumt = [], [], [], []
    for b in range(nb):
        qkv.append(_conv_silu(qkv_ref, buf0_ref, bufout_ref, xe_ref, cw_ref, None, j, c, b))
        small.append(small_ref[b])
        gall = -jnp.exp(galog_ref[...]) * _softplus(small[b] + gdt_ref[...])
        gallt = -jnp.exp(galogc_ref[...]) * _softplus(smallt_ref[b, 0] + gdtc_ref[...])
        gcum.append(_hdot(lower, gall))
        gcumt.append(_hdot(gallt, upper))
    q, k, v, beta, gc, decay = [], [], [], [], [], []
    for b, h in chains:
        qh = qkv[b][:, h * G_DK:(h + 1) * G_DK]
        kh = qkv[b][:, nq + h * G_DK:nq + (h + 1) * G_DK]
        q.append(qh * lax.rsqrt(jnp.sum(qh * qh, axis=-1, keepdims=True) + EPS) * (G_DK ** -0.5))
        k.append(kh * lax.rsqrt(jnp.sum(kh * kh, axis=-1, keepdims=True) + EPS))
        v.append(qkv[b][:, 2 * nq + h * G_DV:2 * nq + (h + 1) * G_DV])
        beta.append(jax.nn.sigmoid(small[b][:, G_B_COL + h:G_B_COL + h + 1]))
        gc.append(gcum[b][:, G_A_COL + h:G_A_COL + h + 1])
        gr = gcumt[b][G_A_COL + h:G_A_COL + h + 1, :]
        decay.append(jnp.exp(jnp.where(incl, gc[-1] - gr, -jnp.inf)))
    n = len(chains)
    kb = [k[i] * beta[i] for i in range(n)]
    amat = [jnp.where(strict, _mm_nt(kb[i], k[i]) * decay[i], 0.0) for i in range(n)]
    tmat = _unit_lower_inv_many(amat, c, r, col)
    u = [_mm(tmat[i], v[i] * beta[i]) for i in range(n)]
    w = [_mm(tmat[i], kb[i] * jnp.exp(gc[i])) for i in range(n)]
    attn = [jnp.where(incl, _mm_nt(q[i], k[i]) * decay[i], 0.0) for i in range(n)]
    s_old = [sout_ref[b, h] for b, h in chains]
    v_new = [u[i] - _mm(w[i], s_old[i]) for i in range(n)]
    o = [_mm(q[i] * jnp.exp(gc[i]), s_old[i]) + _mm(attn[i], v_new[i]) for i in range(n)]
    for i, (b, h) in enumerate(chains):
        glast = gcum[b][c - 1:c, G_A_COL + h:G_A_COL + h + 1]
        sout_ref[b, h] = s_old[i] * jnp.exp(glast) + _mm_tn(k[i] * jnp.exp(glast - gc[i]), v_new[i])
    for b in range(nb):
        z = z_ref[b]
        outs = [_rms(o[b * G_HEADS + h], ng_ref[...]) * _silu(z[:, h * G_DV:(h + 1) * G_DV])
                for h in range(G_HEADS)]
        o_ref[b] = jnp.concatenate(outs, axis=-1)


def _gdn_scan(qkv, z, small, smallt, buf0, s0, consts, c, nb):
    Bsz, L, _ = qkv.shape
    nc = L // c
    assert Bsz % nb == 0
    blk = lambda w: pl.BlockSpec((nb, c, w), lambda b, j: (b, j, 0))
    buf_spec = pl.BlockSpec((nb, G_CONV - 1, G_CONV_CH), lambda b, j: (b, 0, 0))
    state_spec = pl.BlockSpec((nb, G_HEADS, G_DK, G_DV), lambda b, j: (b, 0, 0, 0))
    in_specs = [blk(G_CONV_CH), blk(G_WIDTH), blk(SMALL_W),
                pl.BlockSpec((nb, 1, 16, c), lambda b, j: (b, j, 0, 0)), buf_spec, state_spec]
    in_specs += [_const_spec(a) for a in consts]
    out_shape = [jax.ShapeDtypeStruct((Bsz, L, G_WIDTH), F32),
                 jax.ShapeDtypeStruct((Bsz, G_CONV - 1, G_CONV_CH), F32),
                 jax.ShapeDtypeStruct((Bsz, G_HEADS, G_DK, G_DV), F32)]
    return pl.pallas_call(
        functools.partial(_gdn_kernel, c=c, nb=nb), grid=(Bsz // nb, nc), in_specs=in_specs,
        out_specs=[blk(G_WIDTH), buf_spec, state_spec],
        out_shape=out_shape, scratch_shapes=[pltpu.VMEM((nb, c + 8, G_CONV_CH), F32)],
        compiler_params=_params(("parallel", "arbitrary")), name="gdn_scan")(
            qkv, z, small, smallt, buf0, s0, *consts)


def _out_proj_prompt_body(rows, consts, outs):
    a, m, g, x = rows
    wa, wm, wg = consts
    outs[0][...] = x[...] + _mm(a[...], wa[...]) + _mm(m[...], wm[...]) + _mm(g[...], wg[...])


def _out_proj_sample_body(rows, consts, outs):
    olat, m, g, x = rows
    wvb, wa, wm, wg = consts
    heads = [_mm(olat[:, h * KV_LORA:(h + 1) * KV_LORA], wvb[h]) for h in range(A_HEADS)]
    a = jnp.concatenate(heads, axis=-1)
    outs[0][...] = x[...] + _mm(a, wa[...]) + _mm(m[...], wm[...]) + _mm(g[...], wg[...])


def _top_values(s, k):
    out = []
    for it in range(k):
        m = jnp.max(s, axis=0, keepdims=True)
        out.append(m)
        if it + 1 < k:
            s = jnp.where(s == m, -jnp.inf, s)
    return out


def _peer_query_kernel(x_ref, ln2_ref, wqt_ref, keys_ref, h2t_ref, th1_ref, e1_ref, s2_ref, e2_ref):
    h2t = _rms(x_ref[...], ln2_ref[...]).T.astype(BF16)
    h2t_ref[...] = h2t
    qt = jnp.dot(wqt_ref[...], h2t, preferred_element_type=F32).astype(BF16)
    half = P_DKEY // 2
    for h in range(P_HEADS):
        s1 = jnp.dot(keys_ref[2 * h], qt[(2 * h) * half:(2 * h + 1) * half], preferred_element_type=F32)
        s2 = jnp.dot(keys_ref[2 * h + 1], qt[(2 * h + 1) * half:(2 * h + 2) * half], preferred_element_type=F32)
        v1 = _top_values(s1, P_TOPK)
        v2 = _top_values(s2, P_TOPK)
        v1m = jnp.concatenate(v1, axis=0)
        v2m = jnp.concatenate(v2, axis=0)
        hk = P_TOPK // 2
        cand = jnp.concatenate([v1[0] + v2m] + [v1[a] + v2m[:hk] for a in range(1, hk)] + [v1m[hk:] + v2[0]],
                               axis=0)
        best = _top_values(cand, P_TOPK)
        zsum = jnp.ones_like(best[0])
        for b in best[1:]:
            zsum = zsum + jnp.exp(b - best[0])
        th1_ref[h] = best[P_TOPK - 1] - s1
        s2_ref[h] = s2
        e1_ref[h] = jnp.exp(s1 - v1[0])
        e2_ref[h] = jnp.exp(s2 - v2[0]) / zsum


def _peer_query(x, ln2, wqt, keys, tm):
    T, D = x.shape
    hk = pl.BlockSpec((P_HEADS, P_NKEYS, tm), lambda i: (0, 0, i))
    hk_shape = jax.ShapeDtypeStruct((P_HEADS, P_NKEYS, T), F32)
    return pl.pallas_call(
        _peer_query_kernel, grid=(T // tm,),
        in_specs=[pl.BlockSpec((tm, D), lambda i: (i, 0)), _const_spec(ln2), _const_spec(wqt), _const_spec(keys)],
        out_specs=[pl.BlockSpec((D, tm), lambda i: (0, i)), hk, hk, hk, hk],
        out_shape=[jax.ShapeDtypeStruct((D, T), BF16), hk_shape, hk_shape, hk_shape, hk_shape],
        compiler_params=_params(("parallel",)), name="peer_query")(x, ln2, wqt, keys)


def _gelu(x):
    return 0.5 * x * (1.0 + lax.erf(x * (2.0 ** -0.5)))


def _peer_gate(th1_ref, e1_ref, s2_ref, e2_ref, i1):
    g = None
    for h in range(P_HEADS):
        w = jnp.where(s2_ref[h] >= th1_ref[h, pl.ds(i1, 1), :], e2_ref[h] * e1_ref[h, pl.ds(i1, 1), :], 0.0)
        g = w if g is None else g + w
    return g


def _peer_expert_kernel(h2t_ref, th1_ref, e1_ref, s2_ref, e2_ref, u_ref, vt_ref, y_ref, ga_ref, gb_ref, *, ec):
    j = pl.program_id(1)
    nj = pl.num_programs(1)
    half = ec // 2
    nsub = half // P_NKEYS

    def fill(g_ref, i1_base):
        for r in range(nsub):
            g_ref[r * P_NKEYS:(r + 1) * P_NKEYS, :] = _peer_gate(th1_ref, e1_ref, s2_ref, e2_ref, i1_base + r)

    @pl.when(j == 0)
    def _():
        fill(ga_ref, 0)
        y_ref[...] = jnp.zeros(y_ref.shape, F32)

    pt_a = jnp.dot(u_ref[:half, :], h2t_ref[...], preferred_element_type=F32)
    fill(gb_ref, j * 2 * nsub + nsub)
    act_a = (_gelu(pt_a) * ga_ref[...]).astype(BF16)
    out_a = jnp.dot(vt_ref[:, :half], act_a, preferred_element_type=F32)
    pt_b = jnp.dot(u_ref[half:, :], h2t_ref[...], preferred_element_type=F32)
    fill(ga_ref, jnp.minimum(j + 1, nj - 1) * 2 * nsub)
    act_b = (_gelu(pt_b) * gb_ref[...]).astype(BF16)
    out_b = jnp.dot(vt_ref[:, half:], act_b, preferred_element_type=F32)
    y_ref[...] += out_a + out_b


def _peer_experts(h2t, th1, e1, s2, e2, u, vt, tm, ec):
    D, T = h2t.shape
    ne = u.shape[0]
    hk = pl.BlockSpec((P_HEADS, P_NKEYS, tm), lambda i, j: (0, 0, i))
    return pl.pallas_call(
        functools.partial(_peer_expert_kernel, ec=ec), grid=(T // tm, ne // ec),
        in_specs=[pl.BlockSpec((D, tm), lambda i, j: (0, i)), hk, hk, hk, hk,
                  pl.BlockSpec((ec, D), lambda i, j: (j, 0)), pl.BlockSpec((D, ec), lambda i, j: (0, j))],
        out_specs=pl.BlockSpec((D, tm), lambda i, j: (0, i)), out_shape=jax.ShapeDtypeStruct((D, T), F32),
        scratch_shapes=[pltpu.VMEM((ec // 2, tm), F32), pltpu.VMEM((ec // 2, tm), F32)],
        compiler_params=_params(("parallel", "arbitrary")), name="peer_experts")(
            h2t, th1, e1, s2, e2, u, vt)


def _peer_finish_kernel(x_ref, yt_ref, o_ref):
    o_ref[...] = x_ref[...] + yt_ref[...].T


def _peer_finish(x, yt, tm):
    T, D = x.shape
    tok = pl.BlockSpec((tm, D), lambda i: (i, 0))
    return pl.pallas_call(
        _peer_finish_kernel, grid=(T // tm,), in_specs=[tok, pl.BlockSpec((D, tm), lambda i: (0, i))],
        out_specs=tok, out_shape=jax.ShapeDtypeStruct((T, D), F32),
        compiler_params=_params(("parallel",)), name="peer_finish")(x, yt)


def _row(v, width=None):
    v = v.astype(F32).reshape(1, -1)
    if width is not None and v.shape[1] < width:
        v = jnp.pad(v, ((0, 0), (0, width - v.shape[1])))
    return v


def _layer_weights(i, ln1, w_in, w_out, ln2, q_a_norm, w_qb, kv_a_norm, w_kb, w_vb, qn_nope, qn_rope, kn_nope,
                   kn_rope, m_conv_w, m_conv_b, m_dt_bias, m_A_log, m_D, m_norm, g_conv_w, g_dt_bias, g_A_log,
                   g_norm, p_wq, p_keys, p_u, p_v):
    D = w_in.shape[1]
    o = np.cumsum((0,) + IN_SIZES)
    wi = w_in[i]
    seg = lambda k: wi[:, o[k]:o[k + 1]]
    zeros = lambda n: jnp.zeros((D, n), F32)
    w_kvr = jnp.concatenate([seg(1), seg(2), zeros(LANES - A_ROPE)], axis=1)
    w_small = jnp.concatenate([seg(5), seg(8), seg(9), zeros(SMALL_W - M_HEADS - 2 * G_HEADS)], axis=1)
    in_ws = [w.astype(BF16) for w in (seg(0), w_kvr, seg(3), seg(4), seg(6), seg(7), w_small)]
    wqb = w_qb[i].reshape(Q_LORA, A_HEADS, A_NOPE + A_ROPE)
    w_qn = wqb[:, :, :A_NOPE].reshape(Q_LORA, A_HEADS * A_NOPE).astype(BF16)
    w_qr = wqb[:, :, A_NOPE:].reshape(Q_LORA, A_HEADS * A_ROPE).astype(BF16)
    wkb = w_kb[i]
    wvb = w_vb[i]
    mla_common = [_row(q_a_norm[i]), w_qn, w_qr, _row(qn_nope[i]), _row(jnp.tile(qn_rope[i], A_HEADS)),
                  _row(kv_a_norm[i]), _row(kn_rope[i], LANES), wkb.reshape(KV_LORA, A_HEADS * A_NOPE).astype(BF16),
                  _row(kn_nope[i])]
    mla_prompt = mla_common + [wvb.reshape(KV_LORA, A_HEADS * A_VDIM).astype(BF16)]
    mla_sample = mla_common + [jnp.transpose(wkb, (1, 2, 0)).astype(BF16)]
    wvb_heads = jnp.transpose(wvb, (1, 0, 2)).astype(BF16)
    col16 = lambda v, off: jnp.zeros((16, 1), F32).at[off:off + v.shape[0], 0].set(v.astype(F32))
    ssd = [m_conv_w[i].astype(F32), _row(m_conv_b[i]), _row(m_dt_bias[i], SMALL_W), _row(m_A_log[i], SMALL_W),
           m_dt_bias[i].astype(F32).reshape(M_HEADS, 1), m_A_log[i].astype(F32).reshape(M_HEADS, 1),
           _row(jnp.repeat(m_D[i], M_HEADDIM)), _row(m_norm[i])]
    pad_a = lambda v: jnp.zeros((1, SMALL_W), F32).at[0, G_A_COL:G_A_COL + G_HEADS].set(v.astype(F32))
    gdn = [g_conv_w[i].astype(F32), pad_a(g_dt_bias[i]), pad_a(g_A_log[i]),
           col16(g_dt_bias[i], G_A_COL), col16(g_A_log[i], G_A_COL), _row(g_norm[i])]
    wo = w_out[i].astype(BF16)
    aw = A_HEADS * A_VDIM
    out_ws = [wo[:aw], wo[aw:aw + M_WIDTH], wo[aw + M_WIDTH:]]
    peer = dict(ln2=_row(ln2[i]), wqt=p_wq[i].T.astype(BF16),
                keys=p_keys[i].reshape(2 * P_HEADS, P_NKEYS, P_DKEY // 2).astype(BF16),
                u=p_u[i].astype(BF16), vt=p_v[i].T.astype(BF16))
    return dict(ln1=_row(ln1[i]), in_ws=in_ws, mla_prompt=mla_prompt, mla_sample=mla_sample, wvb_heads=wvb_heads,
                ssd=ssd, gdn=gdn, out_ws=out_ws, peer=peer)


def _rope_tables(pos):
    half = A_ROPE // 2
    inv = ROPE_THETA ** (-jnp.arange(half, dtype=F32) / half)
    ang = pos.astype(F32)[:, None] * inv[None, :]
    cos, sin = jnp.cos(ang), jnp.sin(ang)
    reps = LANES // A_ROPE
    cos_t = jnp.tile(jnp.concatenate([cos, cos], axis=1), (1, reps))
    sin_t = jnp.tile(jnp.concatenate([-sin, sin], axis=1), (1, reps))
    return cos_t, sin_t


def _small_t(small, Bsz, L, c):
    s = small[:, :16].reshape(Bsz, L // c, c, 16)
    return jnp.swapaxes(s, 2, 3)


def _pick_tile(T, cap):
    t = cap
    while T % t:
        t //= 2
    return t


def _token_layer(x, Bsz, L, cos, sin, W, m_buf, m_s, g_buf, g_s, attend, sample):
    T = x.shape[0]
    tm = _pick_tile(T, 256)
    q_lat, kvr, m_z, m_xbc, g_qkv, g_z, small = _in_proj(x, W["ln1"], W["in_ws"], tm)
    mla = _mla_prep(q_lat, kvr, cos, sin, W["mla_sample"] if sample else W["mla_prompt"], tm, sample)
    a_out, cvec, krope, kinv = attend(mla)
    c = CHUNK if L % CHUNK == 0 else L
    smallt = _small_t(small, Bsz, L, c)
    r3 = lambda a: a.reshape(Bsz, L, a.shape[-1])
    m_y, m_buf, m_s = _ssd_scan(r3(m_xbc), r3(m_z), r3(small), smallt, m_buf, m_s, W["ssd"], c)
    g_y, g_buf, g_s = _gdn_scan(r3(g_qkv), r3(g_z), r3(small), smallt, g_buf, g_s, W["gdn"], c,
                                 _pick_tile(Bsz, 4 if sample else 2))
    m_y = m_y.reshape(T, M_WIDTH)
    g_y = g_y.reshape(T, G_WIDTH)
    D = x.shape[1]
    if sample:
        (x,) = _rowwise(_out_proj_sample_body, [a_out, m_y, g_y, x], [W["wvb_heads"]] + W["out_ws"],
                        [(D, F32)], tm, "out_proj_sample")
    else:
        (x,) = _rowwise(_out_proj_prompt_body, [a_out, m_y, g_y, x], W["out_ws"], [(D, F32)], tm, "out_proj_prompt")
    pw = W["peer"]
    h2t, th1, e1, s2, e2 = _peer_query(x, pw["ln2"], pw["wqt"], pw["keys"], tm)
    tp = _pick_tile(T, 512)
    yt = _peer_experts(h2t, th1, e1, s2, e2, pw["u"], pw["vt"], tp, 1024)
    x = _peer_finish(x, yt, tp)
    return x, (cvec, krope, kinv, m_s, m_buf, g_s, g_buf)


def kernel(x_prompt, x_sample, cache_ckv, cache_krope, cache_kscale, state_ssm, state_ssm_conv, state_gdn, state_gdn_conv, page_table, ln1, w_in, w_out, ln2, q_a_norm, w_qb, kv_a_norm, w_kb, w_vb, qn_nope, qn_rope, kn_nope, kn_rope, m_conv_w, m_conv_b, m_dt_bias, m_A_log, m_D, m_norm, g_conv_w, g_dt_bias, g_A_log, g_norm, p_wq, p_keys, p_u, p_v):
    B, S, D = x_prompt.shape
    Bd, Ld, _ = x_sample.shape
    depth = ln1.shape[0]
    n_pages = page_table.shape[1]
    past = n_pages * PAGE_SIZE
    cos_p, sin_p = _rope_tables(jnp.arange(S))
    cos_s, sin_s = _rope_tables(past + jnp.arange(Ld))
    cos_s, sin_s = jnp.tile(cos_s, (Bd, 1)), jnp.tile(sin_s, (Bd, 1))
    npg = _pick_tile(n_pages, 16)
    tq = _pick_tile(S, 512)
    xp = x_prompt.reshape(B * S, D)
    xs = x_sample.reshape(Bd * Ld, D)
    new_p = [[] for _ in range(7)]
    new_s = [[] for _ in range(7)]
    weights = (ln1, w_in, w_out, ln2, q_a_norm, w_qb, kv_a_norm, w_kb, w_vb, qn_nope, qn_rope, kn_nope, kn_rope,
               m_conv_w, m_conv_b, m_dt_bias, m_A_log, m_D, m_norm, g_conv_w, g_dt_bias, g_A_log, g_norm,
               p_wq, p_keys, p_u, p_v)
    for i in range(depth):
        W = _layer_weights(i, *weights)

        def attend_prompt(mla):
            qcat, kcat, v, cvec, krope, kinv = mla
            return _flash_prompt(qcat, kcat, v, B, S, tq), cvec, krope, kinv

        def attend_sample(mla, i=i):
            qa, qr, cvec, krope, kinv = mla
            rows = Ld * A_HEADS
            padk = lambda a: jnp.pad(a.reshape(Bd, Ld, a.shape[-1]), ((0, 0), (0, NEW_PAD - Ld), (0, 0)))
            ksn = jnp.swapaxes(kinv.reshape(Bd, Ld, A_HEADS), 1, 2)
            ksn = jnp.pad(jnp.tile(ksn, (1, Ld, 1)), ((0, 0), (0, 0), (0, NEW_PAD - Ld)))
            o_lat = _decode_attend(page_table, qa.reshape(Bd, rows, KV_LORA), qr.reshape(Bd, rows, A_ROPE),
                                   padk(cvec), padk(krope), ksn, cache_ckv, cache_krope, cache_kscale, i, npg)
            return o_lat.reshape(Bd * Ld, A_HEADS * KV_LORA), cvec, krope, kinv

        zeros = lambda *s: jnp.zeros(s, F32)
        xp, st_p = _token_layer(xp, B, S, cos_p, sin_p, W, zeros(B, M_CONV - 1, M_CONV_CH),
                                zeros(B, M_HEADS, M_HEADDIM, M_STATE), zeros(B, G_CONV - 1, G_CONV_CH),
                                zeros(B, G_HEADS, G_DK, G_DV), attend_prompt, False)
        xs, st_s = _token_layer(xs, Bd, Ld, cos_s, sin_s, W, state_ssm_conv[i], state_ssm[i],
                                state_gdn_conv[i], state_gdn[i], attend_sample, True)
        for lst, val in zip(new_p, st_p):
            lst.append(val)
        for lst, val in zip(new_s, st_s):
            lst.append(val)

    def pack(vals, Bsz, L):
        cvec, krope, kinv, m_s, m_buf, g_s, g_buf = [jnp.stack(v) for v in vals]
        r = lambda a: a.reshape(depth, Bsz, L, a.shape[-1])
        return r(cvec), r(krope), r(kinv), m_s, m_buf, g_s, g_buf

    return (xp.reshape(B, S, D), xs.reshape(Bd, Ld, D)) + pack(new_p, B, S) + pack(new_s, Bd, Ld)
```

```python
import functools
import math

import jax
import jax.numpy as jnp
import numpy as np
from jax import lax
from jax.experimental import pallas as pl
from jax.experimental.pallas import tpu as pltpu

F32 = jnp.float32
BF16 = jnp.bfloat16
EPS = 1e-6

A_HEADS, A_NOPE, A_ROPE, A_VDIM = 8, 128, 64, 128
Q_LORA, KV_LORA = 384, 256
ROPE_THETA = 10000.0
M_HEADS, M_HEADDIM, M_WIDTH, M_GROUPS, M_STATE, M_CONV = 8, 64, 512, 2, 128, 4
M_CONV_CH = M_WIDTH + 2 * M_GROUPS * M_STATE
G_HEADS, G_DK, G_DV, G_WIDTH, G_CONV = 4, 128, 128, 512, 4
G_CONV_CH = 2 * G_HEADS * G_DK + G_WIDTH
IN_SIZES = (Q_LORA, KV_LORA, A_ROPE, M_WIDTH, M_CONV_CH, M_HEADS, G_CONV_CH, G_WIDTH, G_HEADS, G_HEADS)
P_HEADS, P_NKEYS, P_DKEY, P_TOPK = 8, 128, 256, 16
PAGE_SIZE = 128
CHUNK = 64
LANES = 128
SMALL_W = LANES
G_B_COL, G_A_COL = M_HEADS, M_HEADS + G_HEADS
VMEM_LIMIT = 56 * 1024 * 1024
ATT_SCALE = (A_NOPE + A_ROPE) ** -0.5


def _mm(a, b):
    return jnp.dot(a.astype(BF16), b.astype(BF16), preferred_element_type=F32)


def _mm_nt(a, b):
    return lax.dot_general(a.astype(BF16), b.astype(BF16), (((1,), (1,)), ((), ())),
                           preferred_element_type=F32)


def _mm_tn(a, b):
    return lax.dot_general(a.astype(BF16), b.astype(BF16), (((0,), (0,)), ((), ())),
                           preferred_element_type=F32)


def _hdot(a, b):
    return jnp.dot(a, b, precision=lax.Precision.HIGHEST, preferred_element_type=F32)


def _rms(x, g):
    return x * lax.rsqrt(jnp.mean(x * x, axis=-1, keepdims=True) + EPS) * g


def _softplus(x):
    return jnp.maximum(x, 0.0) + jnp.log1p(jnp.exp(-jnp.abs(x)))


def _silu(x):
    return x * jax.nn.sigmoid(x)


def _const_spec(a):
    nd = a.ndim
    return pl.BlockSpec(a.shape, lambda *_: (0,) * nd, pipeline_mode=pl.Buffered(1))


def _params(sem):
    return pltpu.CompilerParams(dimension_semantics=sem, vmem_limit_bytes=VMEM_LIMIT)


def _rowwise(body, rows, consts, outs, tm, name):
    T = rows[0].shape[0]
    assert T % tm == 0
    nr, nc = len(rows), len(consts)

    def kern(*refs):
        body(refs[:nr], refs[nr:nr + nc], refs[nr + nc:])

    in_specs = []
    for a in rows:
        assert a.shape[0] % tm == 0
        per = a.shape[0] // tm
        if a.shape[0] == T:
            in_specs.append(pl.BlockSpec((tm, a.shape[1]), lambda i: (i, 0)))
        else:
            in_specs.append(pl.BlockSpec((tm, a.shape[1]), lambda i, per=per: (i % per, 0)))
    in_specs += [_const_spec(a) for a in consts]
    out_specs = [pl.BlockSpec((tm, c), lambda i: (i, 0)) for c, _ in outs]
    out_shape = [jax.ShapeDtypeStruct((T, c), dt) for c, dt in outs]
    return pl.pallas_call(kern, grid=(T // tm,), in_specs=in_specs, out_specs=out_specs,
                          out_shape=out_shape, compiler_params=_params(("parallel",)),
                          name=name)(*rows, *consts)


def _in_proj_body(rows, consts, outs):
    x = rows[0][...]
    h = _rms(x, consts[0][...]).astype(BF16)
    for w_ref, o_ref in zip(consts[1:], outs):
        o_ref[...] = jnp.dot(h, w_ref[...], preferred_element_type=F32)


def _in_proj(x, ln1, ws, tm):
    outs = [(w.shape[1], F32) for w in ws]
    return _rowwise(_in_proj_body, [x], [ln1] + list(ws), outs, tm, "in_proj")


def _rope_rot(x, cos, sin_signed):
    w = x.shape[-1]
    lane = lax.broadcasted_iota(jnp.int32, x.shape, 1)
    first = (lane & (A_ROPE - 1)) < (A_ROPE // 2)
    rolled = jnp.where(first, pltpu.roll(x, w - A_ROPE // 2, 1), pltpu.roll(x, A_ROPE // 2, 1))
    return x * cos + rolled * sin_signed


def _mla_common(q_lat, kvr, cos, sin, qan, w_qn, w_qr, qn_rope, kvan, kn_rope, w_kb):
    ql = _rms(q_lat, qan).astype(BF16)
    qn = jnp.dot(ql, w_qn, preferred_element_type=F32)
    qr = jnp.dot(ql, w_qr, preferred_element_type=F32)
    wq = qr.shape[1]
    r = lax.broadcasted_iota(jnp.int32, (wq, wq), 0) >> 6
    c = lax.broadcasted_iota(jnp.int32, (wq, wq), 1) >> 6
    head_ones = (r == c).astype(F32)
    ss = _hdot(qr * qr, head_ones)
    qr = qr * lax.rsqrt(ss * (1.0 / A_ROPE) + EPS) * qn_rope
    reps = wq // LANES
    qr = _rope_rot(qr, jnp.tile(cos, (1, reps)), jnp.tile(sin, (1, reps)))
    kv = kvr[:, :KV_LORA]
    kr = kvr[:, KV_LORA:]
    cvec = _rms(kv, kvan)
    krn = kr * lax.rsqrt(jnp.sum(kr * kr, axis=-1, keepdims=True) * (1.0 / A_ROPE) + EPS) * kn_rope
    krope = _rope_rot(krn, cos, sin)
    kraw = jnp.dot(cvec.astype(BF16), w_kb, preferred_element_type=F32)
    return qn, qr, cvec, krope, kraw


def _k_inv_heads(kraw):
    tm = kraw.shape[0]
    lane8 = lax.broadcasted_iota(jnp.int32, (tm, A_HEADS), 1)
    kinv8 = jnp.zeros((tm, A_HEADS), F32)
    invs = []
    for h in range(A_HEADS):
        blk = kraw[:, h * A_NOPE:(h + 1) * A_NOPE]
        inv = lax.rsqrt(jnp.mean(blk * blk, axis=-1, keepdims=True) + EPS)
        invs.append(inv)
        kinv8 = jnp.where(lane8 == h, inv, kinv8)
    return invs, kinv8


def _mla_prompt_kernel(qlat_ref, kvr_ref, cos_ref, sin_ref, qan_ref, wqn_ref, wqr_ref, qnn_ref, qnr_ref,
                       kvan_ref, knr_ref, wkb_ref, knn_ref, wvb_ref,
                       qcat_ref, kcat_ref, v_ref, c_ref, krope_ref, kinv_ref):
    qn, qr, cvec, krope, kraw = _mla_common(
        qlat_ref[...], kvr_ref[...], cos_ref[...], sin_ref[...], qan_ref[...], wqn_ref[...], wqr_ref[...],
        qnr_ref[...], kvan_ref[...], knr_ref[...], wkb_ref[...])
    invs, kinv8 = _k_inv_heads(kraw)
    kr64 = krope[:, :A_ROPE]
    for h in range(A_HEADS):
        qn_h = _rms(qn[:, h * A_NOPE:(h + 1) * A_NOPE], qnn_ref[...])
        qcat_ref[h] = jnp.concatenate([qn_h, qr[:, h * A_ROPE:(h + 1) * A_ROPE]], axis=-1).astype(BF16)
        kn_h = kraw[:, h * A_NOPE:(h + 1) * A_NOPE] * invs[h] * knn_ref[...]
        kcat_ref[h] = jnp.concatenate([kn_h, kr64], axis=-1).astype(BF16)
    v_ref[...] = jnp.dot(cvec.astype(BF16), wvb_ref[...], preferred_element_type=F32).astype(BF16)
    c_ref[...] = cvec
    krope_ref[...] = kr64
    kinv_ref[...] = kinv8


def _mla_sample_kernel(qlat_ref, kvr_ref, cos_ref, sin_ref, qan_ref, wqn_ref, wqr_ref, qnn_ref, qnr_ref,
                       kvan_ref, knr_ref, wkb_ref, knn_ref, wkbt_ref,
                       qa_ref, qr_ref, c_ref, krope_ref, kinv_ref):
    qn, qr, cvec, krope, kraw = _mla_common(
        qlat_ref[...], kvr_ref[...], cos_ref[...], sin_ref[...], qan_ref[...], wqn_ref[...], wqr_ref[...],
        qnr_ref[...], kvan_ref[...], knr_ref[...], wkb_ref[...])
    _, kinv8 = _k_inv_heads(kraw)
    for h in range(A_HEADS):
        qn_h = _rms(qn[:, h * A_NOPE:(h + 1) * A_NOPE], qnn_ref[...]) * knn_ref[...]
        qa_ref[:, h * KV_LORA:(h + 1) * KV_LORA] = jnp.dot(
            qn_h.astype(BF16), wkbt_ref[h], preferred_element_type=F32).astype(BF16)
    qr_ref[...] = qr.astype(BF16)
    c_ref[...] = cvec
    krope_ref[...] = krope[:, :A_ROPE]
    kinv_ref[...] = kinv8


def _mla_prep(q_lat, kvr, cos, sin, consts, tm, sample):
    T = q_lat.shape[0]
    rows = [q_lat, kvr, cos, sin]
    in_specs = []
    for a in rows:
        per = a.shape[0] // tm
        if a.shape[0] == T:
            in_specs.append(pl.BlockSpec((tm, a.shape[1]), lambda i: (i, 0)))
        else:
            in_specs.append(pl.BlockSpec((tm, a.shape[1]), lambda i, per=per: (i % per, 0)))
    in_specs += [_const_spec(a) for a in consts]
    row_out = lambda c: pl.BlockSpec((tm, c), lambda i: (i, 0))
    tail_shapes = [jax.ShapeDtypeStruct((T, KV_LORA), F32), jax.ShapeDtypeStruct((T, A_ROPE), F32),
                   jax.ShapeDtypeStruct((T, A_HEADS), F32)]
    tail_specs = [row_out(KV_LORA), row_out(A_ROPE), row_out(A_HEADS)]
    if sample:
        kern = _mla_sample_kernel
        out_shape = [jax.ShapeDtypeStruct((T, A_HEADS * KV_LORA), BF16),
                     jax.ShapeDtypeStruct((T, A_HEADS * A_ROPE), BF16)] + tail_shapes
        out_specs = [row_out(A_HEADS * KV_LORA), row_out(A_HEADS * A_ROPE)] + tail_specs
    else:
        kern = _mla_prompt_kernel
        dqk = A_NOPE + A_ROPE
        head_spec = pl.BlockSpec((A_HEADS, tm, dqk), lambda i: (0, i, 0))
        out_shape = [jax.ShapeDtypeStruct((A_HEADS, T, dqk), BF16), jax.ShapeDtypeStruct((A_HEADS, T, dqk), BF16),
                     jax.ShapeDtypeStruct((T, A_HEADS * A_VDIM), BF16)] + tail_shapes
        out_specs = [head_spec, head_spec, row_out(A_HEADS * A_VDIM)] + tail_specs
    return pl.pallas_call(kern, grid=(T // tm,), in_specs=in_specs, out_specs=out_specs, out_shape=out_shape,
                          compiler_params=_params(("parallel",)),
                          name="mla_prep_sample" if sample else "mla_prep_prompt")(*rows, *consts)


def _flash_kernel(q_ref, k_ref, v_ref, o_ref, *, tq):
    i = pl.program_id(2)
    q = q_ref[0]
    row = lax.broadcasted_iota(jnp.int32, (tq, tq), 0)
    col = lax.broadcasted_iota(jnp.int32, (tq, tq), 1)

    def step(j, carry, diagonal):
        m, l, acc = carry
        start = pl.multiple_of(j * tq, tq)
        k = k_ref[0, pl.ds(start, tq), :]
        v = v_ref[pl.ds(start, tq), :]
        s = lax.dot_general(q, k, (((1,), (1,)), ((), ())), preferred_element_type=F32) * ATT_SCALE
        if diagonal:
            s = jnp.where(col <= row, s, -jnp.inf)
        m_new = jnp.maximum(m, jnp.max(s, axis=-1, keepdims=True))
        alpha = jnp.exp(m - m_new)
        p = jnp.exp(s - m_new)
        l = alpha * l + jnp.sum(p, axis=-1, keepdims=True)
        acc = alpha * acc + jnp.dot(p.astype(BF16), v, preferred_element_type=F32)
        return m_new, l, acc

    init = (jnp.full((tq, 1), -jnp.inf, F32), jnp.zeros((tq, 1), F32), jnp.zeros((tq, A_VDIM), F32))
    carry = lax.fori_loop(0, i, lambda j, c: step(j, c, False), init)
    _, l, acc = step(i, carry, True)
    o_ref[...] = (acc / l).astype(o_ref.dtype)


def _flash_prompt(qcat, kcat, v, Bsz, S, tq):
    nq = S // tq
    dqk = A_NOPE + A_ROPE
    T = Bsz * S
    return pl.pallas_call(
        functools.partial(_flash_kernel, tq=tq),
        grid=(Bsz, A_HEADS, nq),
        in_specs=[pl.BlockSpec((1, tq, dqk), lambda b, h, i: (h, b * nq + i, 0)),
                  pl.BlockSpec((1, S, dqk), lambda b, h, i: (h, b, 0)),
                  pl.BlockSpec((S, A_VDIM), lambda b, h, i: (b, h))],
        out_specs=pl.BlockSpec((tq, A_VDIM), lambda b, h, i: (b * nq + i, h)),
        out_shape=jax.ShapeDtypeStruct((T, A_HEADS * A_VDIM), BF16),
        compiler_params=_params(("parallel", "parallel", "arbitrary")),
        name="flash_prompt")(qcat, kcat, v)


NEW_PAD = 16


def _decode_kernel(pt_ref, qa_ref, qr_ref, cn_ref, krn_ref, ksn_ref, ckv_hbm, krt_hbm, kst_hbm, o_ref,
                   ck_buf, kr_buf, ks_buf, sem, *, li, npg):
    b = pl.program_id(0)
    nseq = pl.num_programs(0)
    rows = qa_ref.shape[1]
    reps = rows // A_HEADS
    ngroups = pt_ref.shape[1] // npg

    def group_copies(seq, g, slot):
        cps = []
        for p in range(npg):
            pid = pt_ref[seq, g * npg + p]
            cps.append(pltpu.make_async_copy(ckv_hbm.at[li, pid], ck_buf.at[slot, p], sem.at[slot, 0]))
            cps.append(pltpu.make_async_copy(krt_hbm.at[li, pid], kr_buf.at[slot, p], sem.at[slot, 1]))
            cps.append(pltpu.make_async_copy(kst_hbm.at[li, pid], ks_buf.at[slot, p], sem.at[slot, 2]))
        return cps

    @pl.when(b == 0)
    def _():
        for cp in group_copies(0, 0, 0):
            cp.start()

    qa = qa_ref[0]
    qr = qr_ref[0]
    nt = (((1,), (1,)), ((), ()))

    def body(g, carry):
        m_old, l_old, acc_old = carry
        gg = b * ngroups + g
        slot = gg & 1
        last = g + 1 == ngroups
        nxt_seq = jnp.where(last, b + 1, b)
        nxt_g = jnp.where(last, 0, g + 1)

        @pl.when(gg + 1 < nseq * ngroups)
        def _():
            for cp in group_copies(nxt_seq, nxt_g, 1 - slot):
                cp.start()

        for cp in group_copies(b, g, slot):
            cp.wait()
        ss, ccs = [], []
        for p in range(npg):
            cc = ck_buf[slot, p].astype(BF16)
            s = lax.dot_general(qa, cc, nt, preferred_element_type=F32)
            s = s * jnp.tile(ks_buf[slot, p], (reps, 1))
            s = s + jnp.dot(qr, kr_buf[slot, p].astype(BF16), preferred_element_type=F32)
            ss.append(s * ATT_SCALE)
            ccs.append(cc)
        smax = ss[0]
        for s in ss[1:]:
            smax = jnp.maximum(smax, s)
        m_new = jnp.maximum(m_old, jnp.max(smax, axis=-1, keepdims=True))
        alpha = jnp.exp(m_old - m_new)
        acc = alpha * acc_old
        esum = None
        for s, cc in zip(ss, ccs):
            e = jnp.exp(s - m_new)
            esum = e if esum is None else esum + e
            acc = acc + jnp.dot(e.astype(BF16), cc, preferred_element_type=F32)
        l_new = alpha * l_old + jnp.sum(esum, axis=-1, keepdims=True)
        return m_new, l_new, acc

    init = (jnp.full((rows, 1), -jnp.inf, F32), jnp.zeros((rows, 1), F32), jnp.zeros((rows, KV_LORA), F32))
    m1, l1, acc1 = lax.fori_loop(0, ngroups, body, init)

    cn = cn_ref[0].astype(BF16)
    s = lax.dot_general(qa, cn, nt, preferred_element_type=F32) * ksn_ref[0]
    s = s + lax.dot_general(qr, krn_ref[0].astype(BF16), nt, preferred_element_type=F32)
    row = lax.broadcasted_iota(jnp.int32, s.shape, 0)
    col = lax.broadcasted_iota(jnp.int32, s.shape, 1)
    s = jnp.where(col <= (row >> 3), s * ATT_SCALE, -jnp.inf)
    m2 = jnp.maximum(m1, jnp.max(s, axis=-1, keepdims=True))
    a2 = jnp.exp(m1 - m2)
    e = jnp.exp(s - m2)
    l2 = a2 * l1 + jnp.sum(e, axis=-1, keepdims=True)
    acc2 = a2 * acc1 + jnp.dot(e.astype(BF16), cn, preferred_element_type=F32)
    o_ref[0] = acc2 / l2


def _decode_attend(page_table, qa, qr, cn, krn, ksn, cache_ckv, cache_krope_t, cache_kscale_t, li, npg):
    Bd, n_pages = page_table.shape
    rows = qa.shape[1]
    assert n_pages % npg == 0
    per_b = lambda b, pt: (b, 0, 0)
    hbm = pl.BlockSpec(memory_space=pl.ANY)
    in_specs = [pl.BlockSpec((1, rows, KV_LORA), per_b), pl.BlockSpec((1, rows, A_ROPE), per_b),
                pl.BlockSpec((1, NEW_PAD, KV_LORA), per_b), pl.BlockSpec((1, NEW_PAD, A_ROPE), per_b),
                pl.BlockSpec((1, rows, NEW_PAD), per_b), hbm, hbm, hbm]
    grid_spec = pltpu.PrefetchScalarGridSpec(
        num_scalar_prefetch=1, grid=(Bd,), in_specs=in_specs,
        out_specs=pl.BlockSpec((1, rows, KV_LORA), per_b),
        scratch_shapes=[pltpu.VMEM((2, npg, PAGE_SIZE, KV_LORA), F32), pltpu.VMEM((2, npg, A_ROPE, PAGE_SIZE), F32),
                        pltpu.VMEM((2, npg, A_HEADS, PAGE_SIZE), F32), pltpu.SemaphoreType.DMA((2, 3))])
    return pl.pallas_call(
        functools.partial(_decode_kernel, li=li, npg=npg), grid_spec=grid_spec,
        out_shape=jax.ShapeDtypeStruct((Bd, rows, KV_LORA), F32),
        compiler_params=_params(("arbitrary",)), name="decode_attend")(
            page_table, qa, qr, cn, krn, ksn, cache_ckv, cache_krope_t, cache_kscale_t)


def _tri_masks(c):
    r = lax.broadcasted_iota(jnp.int32, (c, c), 0)
    col = lax.broadcasted_iota(jnp.int32, (c, c), 1)
    return r, col


def _conv_silu(x_ref, buf0_ref, bufout_ref, xe_ref, w_ref, bias, j, c, b=0):
    @pl.when(j == 0)
    def _():
        xe_ref[b, 5:8, :] = buf0_ref[b]

    xe_ref[b, 8:8 + c, :] = x_ref[b]
    y = xe_ref[b, pl.ds(5, c), :] * w_ref[0:1, :]
    for k in range(1, 4):
        y = y + xe_ref[b, pl.ds(5 + k, c), :] * w_ref[k:k + 1, :]
    if bias is not None:
        y = y + bias
    bufout_ref[b] = xe_ref[b, c + 5:c + 8, :]
    tail = xe_ref[b, c:c + 8, :]
    xe_ref[b, 0:8, :] = tail
    return _silu(y)


def _ssd_kernel(xbc_ref, z_ref, small_ref, smallt_ref, buf0_ref, s0_ref, cw_ref, cb_ref, dtb_ref, alog_ref,
                dtbc_ref, alogc_ref, dskip_ref, ng_ref, y_ref, bufout_ref, sout_ref, xe_ref, *, c):
    j = pl.program_id(1)

    @pl.when(j == 0)
    def _():
        sout_ref[0] = s0_ref[0]

    xbc = _conv_silu(xbc_ref, buf0_ref, bufout_ref, xe_ref, cw_ref, cb_ref[...], j, c)
    xs = xbc[:, :M_WIDTH]
    gw = M_GROUPS * M_STATE
    bm = xbc[:, M_WIDTH:M_WIDTH + gw]
    cm = xbc[:, M_WIDTH + gw:]
    dt = _softplus(small_ref[0] + dtb_ref[...])
    a = dt * (-jnp.exp(alog_ref[...]))
    dtt = _softplus(smallt_ref[0, 0][:M_HEADS] + dtbc_ref[...])
    at = dtt * (-jnp.exp(alogc_ref[...]))
    r, col = _tri_masks(c)
    tri = col <= r
    cum = _hdot(tri.astype(F32), a)
    cumt = _hdot(at, (r <= col).astype(F32))
    rep = M_HEADS // M_GROUPS
    cb = [_mm_nt(cm[:, g * M_STATE:(g + 1) * M_STATE], bm[:, g * M_STATE:(g + 1) * M_STATE])
          for g in range(M_GROUPS)]
    ys = []
    for h in range(M_HEADS):
        g = h // rep
        cc = cum[:, h:h + 1]
        cr = cumt[h:h + 1, :]
        lmat = jnp.exp(jnp.where(tri, cc - cr, -jnp.inf))
        mmat = cb[g] * lmat * dtt[h:h + 1, :]
        xh = xs[:, h * M_HEADDIM:(h + 1) * M_HEADDIM]
        bg = bm[:, g * M_STATE:(g + 1) * M_STATE]
        cg = cm[:, g * M_STATE:(g + 1) * M_STATE]
        s_h = sout_ref[0, h]
        y = _mm(mmat, xh) + _mm_nt(cg, s_h) * jnp.exp(cc)
        clast = cum[c - 1:c, h:h + 1]
        to_end = jnp.exp(clast - cc) * dt[:, h:h + 1]
        sout_ref[0, h] = s_h * jnp.exp(clast) + _mm_tn(xh * to_end, bg)
        ys.append(y)
    y = jnp.concatenate(ys, axis=-1) + xs * dskip_ref[...]
    y = y * _silu(z_ref[0])
    y_ref[0] = _rms(y, ng_ref[...])


def _ssd_scan(xbc, z, small, smallt, buf0, s0, consts, c):
    Bsz, L, _ = xbc.shape
    nc = L // c
    blk = lambda w: pl.BlockSpec((1, c, w), lambda b, j: (b, j, 0))
    in_specs = [blk(M_CONV_CH), blk(M_WIDTH), blk(SMALL_W),
                pl.BlockSpec((1, 1, 16, c), lambda b, j: (b, j, 0, 0)),
                pl.BlockSpec((1, M_CONV - 1, M_CONV_CH), lambda b, j: (b, 0, 0)),
                pl.BlockSpec((1, M_HEADS, M_HEADDIM, M_STATE), lambda b, j: (b, 0, 0, 0))]
    in_specs += [_const_spec(a) for a in consts]
    out_specs = [blk(M_WIDTH),
                 pl.BlockSpec((1, M_CONV - 1, M_CONV_CH), lambda b, j: (b, 0, 0)),
                 pl.BlockSpec((1, M_HEADS, M_HEADDIM, M_STATE), lambda b, j: (b, 0, 0, 0))]
    out_shape = [jax.ShapeDtypeStruct((Bsz, L, M_WIDTH), F32),
                 jax.ShapeDtypeStruct((Bsz, M_CONV - 1, M_CONV_CH), F32),
                 jax.ShapeDtypeStruct((Bsz, M_HEADS, M_HEADDIM, M_STATE), F32)]
    return pl.pallas_call(
        functools.partial(_ssd_kernel, c=c), grid=(Bsz, nc), in_specs=in_specs, out_specs=out_specs,
        out_shape=out_shape, scratch_shapes=[pltpu.VMEM((1, c + 8, M_CONV_CH), F32)],
        compiler_params=_params(("parallel", "arbitrary")), name="ssd_scan")(
            xbc, z, small, smallt, buf0, s0, *consts)


def _unit_lower_inv_many(amats, c, r, col):
    eye = (r == col).astype(F32)
    same8 = (r >> 3) == (col >> 3)
    a8 = [jnp.where(same8, a, 0.0) for a in amats]
    x = [eye - a for a in a8]
    p = [_hdot(a, a) for a in a8]
    x = [xi + _hdot(xi, pi) for xi, pi in zip(x, p)]
    p = [_hdot(pi, pi) for pi in p]
    x = [xi + _hdot(xi, pi) for xi, pi in zip(x, p)]
    b = 8
    while b < c:
        sh = int(math.log2(b))
        pick = ((r >> (sh + 1)) == (col >> (sh + 1))) & ((r >> sh) != (col >> sh))
        low = [jnp.where(pick, a, 0.0) for a in amats]
        t = [_hdot(xi, li) for xi, li in zip(x, low)]
        x = [xi - _hdot(ti, xi) for xi, ti in zip(x, t)]
        b *= 2
    return x


def _gdn_kernel(qkv_ref, z_ref, small_ref, smallt_ref, buf0_ref, s0_ref, cw_ref, gdt_ref, galog_ref,
                gdtc_ref, galogc_ref, ng_ref, o_ref, bufout_ref, sout_ref, xe_ref, *, c, nb):
    j = pl.program_id(1)

    @pl.when(j == 0)
    def _():
        sout_ref[...] = s0_ref[...]

    nq = G_HEADS * G_DK
    r, col = _tri_masks(c)
    incl = col <= r
    strict = col < r
    lower = incl.astype(F32)
    upper = (r <= col).astype(F32)
    chains = [(b, h) for b in range(nb) for h in range(G_HEADS)]
    qkv, small, gcum, gcumt = [], [], [], []
    for b in range(nb):
        qkv.append(_conv_silu(qkv_ref, buf0_ref, bufout_ref, xe_ref, cw_ref, None, j, c, b))
        small.append(small_ref[b])
        gall = -jnp.exp(galog_ref[...]) * _softplus(small[b] + gdt_ref[...])
        gallt = -jnp.exp(galogc_ref[...]) * _softplus(smallt_ref[b, 0] + gdtc_ref[...])
        gcum.append(_hdot(lower, gall))
        gcumt.append(_hdot(gallt, upper))
    q, k, v, beta, gc, decay = [], [], [], [], [], []
    for b, h in chains:
        qh = qkv[b][:, h * G_DK:(h + 1) * G_DK]
        kh = qkv[b][:, nq + h * G_DK:nq + (h + 1) * G_DK]
        q.append(qh * lax.rsqrt(jnp.sum(qh * qh, axis=-1, keepdims=True) + EPS) * (G_DK ** -0.5))
        k.append(kh * lax.rsqrt(jnp.sum(kh * kh, axis=-1, keepdims=True) + EPS))
        v.append(qkv[b][:, 2 * nq + h * G_DV:2 * nq + (h + 1) * G_DV])
        beta.append(jax.nn.sigmoid(small[b][:, G_B_COL + h:G_B_COL + h + 1]))
        gc.append(gcum[b][:, G_A_COL + h:G_A_COL + h + 1])
        gr = gcumt[b][G_A_COL + h:G_A_COL + h + 1, :]
        decay.append(jnp.exp(jnp.where(incl, gc[-1] - gr, -jnp.inf)))
    n = len(chains)
    kb = [k[i] * beta[i] for i in range(n)]
    amat = [jnp.where(strict, _mm_nt(kb[i], k[i]) * decay[i], 0.0) for i in range(n)]
    tmat = _unit_lower_inv_many(amat, c, r, col)
    u = [_mm(tmat[i], v[i] * beta[i]) for i in range(n)]
    w = [_mm(tmat[i], kb[i] * jnp.exp(gc[i])) for i in range(n)]
    attn = [jnp.where(incl, _mm_nt(q[i], k[i]) * decay[i], 0.0) for i in range(n)]
    s_old = [sout_ref[b, h] for b, h in chains]
    v_new = [u[i] - _mm(w[i], s_old[i]) for i in range(n)]
    o = [_mm(q[i] * jnp.exp(gc[i]), s_old[i]) + _mm(attn[i], v_new[i]) for i in range(n)]
    for i, (b, h) in enumerate(chains):
        glast = gcum[b][c - 1:c, G_A_COL + h:G_A_COL + h + 1]
        sout_ref[b, h] = s_old[i] * jnp.exp(glast) + _mm_tn(k[i] * jnp.exp(glast - gc[i]), v_new[i])
    for b in range(nb):
        z = z_ref[b]
        outs = [_rms(o[b * G_HEADS + h], ng_ref[...]) * _silu(z[:, h * G_DV:(h + 1) * G_DV])
                for h in range(G_HEADS)]
        o_ref[b] = jnp.concatenate(outs, axis=-1)


def _gdn_scan(qkv, z, small, smallt, buf0, s0, consts, c, nb):
    Bsz, L, _ = qkv.shape
    nc = L // c
    assert Bsz % nb == 0
    blk = lambda w: pl.BlockSpec((nb, c, w), lambda b, j: (b, j, 0))
    buf_spec = pl.BlockSpec((nb, G_CONV - 1, G_CONV_CH), lambda b, j: (b, 0, 0))
    state_spec = pl.BlockSpec((nb, G_HEADS, G_DK, G_DV), lambda b, j: (b, 0, 0, 0))
    in_specs = [blk(G_CONV_CH), blk(G_WIDTH), blk(SMALL_W),
                pl.BlockSpec((nb, 1, 16, c), lambda b, j: (b, j, 0, 0)), buf_spec, state_spec]
    in_specs += [_const_spec(a) for a in consts]
    out_shape = [jax.ShapeDtypeStruct((Bsz, L, G_WIDTH), F32),
                 jax.ShapeDtypeStruct((Bsz, G_CONV - 1, G_CONV_CH), F32),
                 jax.ShapeDtypeStruct((Bsz, G_HEADS, G_DK, G_DV), F32)]
    return pl.pallas_call(
        functools.partial(_gdn_kernel, c=c, nb=nb), grid=(Bsz // nb, nc), in_specs=in_specs,
        out_specs=[blk(G_WIDTH), buf_spec, state_spec],
        out_shape=out_shape, scratch_shapes=[pltpu.VMEM((nb, c + 8, G_CONV_CH), F32)],
        compiler_params=_params(("parallel", "arbitrary")), name="gdn_scan")(
            qkv, z, small, smallt, buf0, s0, *consts)


def _out_proj_prompt_body(rows, consts, outs):
    a, m, g, x = rows
    wa, wm, wg = consts
    outs[0][...] = x[...] + _mm(a[...], wa[...]) + _mm(m[...], wm[...]) + _mm(g[...], wg[...])


def _out_proj_sample_body(rows, consts, outs):
    olat, m, g, x = rows
    wvb, wa, wm, wg = consts
    heads = [_mm(olat[:, h * KV_LORA:(h + 1) * KV_LORA], wvb[h]) for h in range(A_HEADS)]
    a = jnp.concatenate(heads, axis=-1)
    outs[0][...] = x[...] + _mm(a, wa[...]) + _mm(m[...], wm[...]) + _mm(g[...], wg[...])


def _top_values(s, k):
    out = []
    for it in range(k):
        m = jnp.max(s, axis=0, keepdims=True)
        out.append(m)
        if it + 1 < k:
            s = jnp.where(s == m, -jnp.inf, s)
    return out


def _peer_query_kernel(x_ref, ln2_ref, wqt_ref, keys_ref, h2t_ref, th1_ref, e1_ref, s2_ref, e2_ref):
    h2t = _rms(x_ref[...], ln2_ref[...]).T.astype(BF16)
    h2t_ref[...] = h2t
    qt = jnp.dot(wqt_ref[...], h2t, preferred_element_type=F32).astype(BF16)
    half = P_DKEY // 2
    for h in range(P_HEADS):
        s1 = jnp.dot(keys_ref[2 * h], qt[(2 * h) * half:(2 * h + 1) * half], preferred_element_type=F32)
        s2 = jnp.dot(keys_ref[2 * h + 1], qt[(2 * h + 1) * half:(2 * h + 2) * half], preferred_element_type=F32)
        v1 = _top_values(s1, P_TOPK)
        v2 = _top_values(s2, P_TOPK)
        v1m = jnp.concatenate(v1, axis=0)
        v2m = jnp.concatenate(v2, axis=0)
        hk = P_TOPK // 2
        cand = jnp.concatenate([v1[0] + v2m] + [v1[a] + v2m[:hk] for a in range(1, hk)] + [v1m[hk:] + v2[0]],
                               axis=0)
        best = _top_values(cand, P_TOPK)
        zsum = jnp.ones_like(best[0])
        for b in best[1:]:
            zsum = zsum + jnp.exp(b - best[0])
        th1_ref[h] = best[P_TOPK - 1] - s1
        s2_ref[h] = s2
        e1_ref[h] = jnp.exp(s1 - v1[0])
        e2_ref[h] = jnp.exp(s2 - v2[0]) / zsum


def _peer_query(x, ln2, wqt, keys, tm):
    T, D = x.shape
    hk = pl.BlockSpec((P_HEADS, P_NKEYS, tm), lambda i: (0, 0, i))
    hk_shape = jax.ShapeDtypeStruct((P_HEADS, P_NKEYS, T), F32)
    return pl.pallas_call(
        _peer_query_kernel, grid=(T // tm,),
        in_specs=[pl.BlockSpec((tm, D), lambda i: (i, 0)), _const_spec(ln2), _const_spec(wqt), _const_spec(keys)],
        out_specs=[pl.BlockSpec((D, tm), lambda i: (0, i)), hk, hk, hk, hk],
        out_shape=[jax.ShapeDtypeStruct((D, T), BF16), hk_shape, hk_shape, hk_shape, hk_shape],
        compiler_params=_params(("parallel",)), name="peer_query")(x, ln2, wqt, keys)


def _gelu(x):
    return 0.5 * x * (1.0 + lax.erf(x * (2.0 ** -0.5)))


def _peer_gate(th1_ref, e1_ref, s2_ref, e2_ref, i1):
    g = None
    for h in range(P_HEADS):
        w = jnp.where(s2_ref[h] >= th1_ref[h, pl.ds(i1, 1), :], e2_ref[h] * e1_ref[h, pl.ds(i1, 1), :], 0.0)
        g = w if g is None else g + w
    return g


def _peer_expert_kernel(h2t_ref, th1_ref, e1_ref, s2_ref, e2_ref, u_ref, vt_ref, y_ref, ga_ref, gb_ref, *, ec):
    j = pl.program_id(1)
    nj = pl.num_programs(1)
    half = ec // 2
    nsub = half // P_NKEYS

    def fill(g_ref, i1_base):
        for r in range(nsub):
            g_ref[r * P_NKEYS:(r + 1) * P_NKEYS, :] = _peer_gate(th1_ref, e1_ref, s2_ref, e2_ref, i1_base + r)

    @pl.when(j == 0)
    def _():
        fill(ga_ref, 0)
        y_ref[...] = jnp.zeros(y_ref.shape, F32)

    pt_a = jnp.dot(u_ref[:half, :], h2t_ref[...], preferred_element_type=F32)
    fill(gb_ref, j * 2 * nsub + nsub)
    act_a = (_gelu(pt_a) * ga_ref[...]).astype(BF16)
    out_a = jnp.dot(vt_ref[:, :half], act_a, preferred_element_type=F32)
    pt_b = jnp.dot(u_ref[half:, :], h2t_ref[...], preferred_element_type=F32)
    fill(ga_ref, jnp.minimum(j + 1, nj - 1) * 2 * nsub)
    act_b = (_gelu(pt_b) * gb_ref[...]).astype(BF16)
    out_b = jnp.dot(vt_ref[:, half:], act_b, preferred_element_type=F32)
    y_ref[...] += out_a + out_b


def _peer_experts(h2t, th1, e1, s2, e2, u, vt, tm, ec):
    D, T = h2t.shape
    ne = u.shape[0]
    hk = pl.BlockSpec((P_HEADS, P_NKEYS, tm), lambda i, j: (0, 0, i))
    return pl.pallas_call(
        functools.partial(_peer_expert_kernel, ec=ec), grid=(T // tm, ne // ec),
        in_specs=[pl.BlockSpec((D, tm), lambda i, j: (0, i)), hk, hk, hk, hk,
                  pl.BlockSpec((ec, D), lambda i, j: (j, 0)), pl.BlockSpec((D, ec), lambda i, j: (0, j))],
        out_specs=pl.BlockSpec((D, tm), lambda i, j: (0, i)), out_shape=jax.ShapeDtypeStruct((D, T), F32),
        scratch_shapes=[pltpu.VMEM((ec // 2, tm), F32), pltpu.VMEM((ec // 2, tm), F32)],
        compiler_params=_params(("parallel", "arbitrary")), name="peer_experts")(
            h2t, th1, e1, s2, e2, u, vt)


def _peer_finish_kernel(x_ref, yt_ref, o_ref):
    o_ref[...] = x_ref[...] + yt_ref[...].T


def _peer_finish(x, yt, tm):
    T, D = x.shape
    tok = pl.BlockSpec((tm, D), lambda i: (i, 0))
    return pl.pallas_call(
        _peer_finish_kernel, grid=(T // tm,), in_specs=[tok, pl.BlockSpec((D, tm), lambda i: (0, i))],
        out_specs=tok, out_shape=jax.ShapeDtypeStruct((T, D), F32),
        compiler_params=_params(("parallel",)), name="peer_finish")(x, yt)


def _row(v, width=None):
    v = v.astype(F32).reshape(1, -1)
    if width is not None and v.shape[1] < width:
        v = jnp.pad(v, ((0, 0), (0, width - v.shape[1])))
    return v


def _layer_weights(i, ln1, w_in, w_out, ln2, q_a_norm, w_qb, kv_a_norm, w_kb, w_vb, qn_nope, qn_rope, kn_nope,
                   kn_rope, m_conv_w, m_conv_b, m_dt_bias, m_A_log, m_D, m_norm, g_conv_w, g_dt_bias, g_A_log,
                   g_norm, p_wq, p_keys, p_u, p_v):
    D = w_in.shape[1]
    o = np.cumsum((0,) + IN_SIZES)
    wi = w_in[i]
    seg = lambda k: wi[:, o[k]:o[k + 1]]
    zeros = lambda n: jnp.zeros((D, n), F32)
    w_kvr = jnp.concatenate([seg(1), seg(2), zeros(LANES - A_ROPE)], axis=1)
    w_small = jnp.concatenate([seg(5), seg(8), seg(9), zeros(SMALL_W - M_HEADS - 2 * G_HEADS)], axis=1)
    in_ws = [w.astype(BF16) for w in (seg(0), w_kvr, seg(3), seg(4), seg(6), seg(7), w_small)]
    wqb = w_qb[i].reshape(Q_LORA, A_HEADS, A_NOPE + A_ROPE)
    w_qn = wqb[:, :, :A_NOPE].reshape(Q_LORA, A_HEADS * A_NOPE).astype(BF16)
    w_qr = wqb[:, :, A_NOPE:].reshape(Q_LORA, A_HEADS * A_ROPE).astype(BF16)
    wkb = w_kb[i]
    wvb = w_vb[i]
    mla_common = [_row(q_a_norm[i]), w_qn, w_qr, _row(qn_nope[i]), _row(jnp.tile(qn_rope[i], A_HEADS)),
                  _row(kv_a_norm[i]), _row(kn_rope[i], LANES), wkb.reshape(KV_LORA, A_HEADS * A_NOPE).astype(BF16),
                  _row(kn_nope[i])]
    mla_prompt = mla_common + [wvb.reshape(KV_LORA, A_HEADS * A_VDIM).astype(BF16)]
    mla_sample = mla_common + [jnp.transpose(wkb, (1, 2, 0)).astype(BF16)]
    wvb_heads = jnp.transpose(wvb, (1, 0, 2)).astype(BF16)
    col16 = lambda v, off: jnp.zeros((16, 1), F32).at[off:off + v.shape[0], 0].set(v.astype(F32))
    ssd = [m_conv_w[i].astype(F32), _row(m_conv_b[i]), _row(m_dt_bias[i], SMALL_W), _row(m_A_log[i], SMALL_W),
           m_dt_bias[i].astype(F32).reshape(M_HEADS, 1), m_A_log[i].astype(F32).reshape(M_HEADS, 1),
           _row(jnp.repeat(m_D[i], M_HEADDIM)), _row(m_norm[i])]
    pad_a = lambda v: jnp.zeros((1, SMALL_W), F32).at[0, G_A_COL:G_A_COL + G_HEADS].set(v.astype(F32))
    gdn = [g_conv_w[i].astype(F32), pad_a(g_dt_bias[i]), pad_a(g_A_log[i]),
           col16(g_dt_bias[i], G_A_COL), col16(g_A_log[i], G_A_COL), _row(g_norm[i])]
    wo = w_out[i].astype(BF16)
    aw = A_HEADS * A_VDIM
    out_ws = [wo[:aw], wo[aw:aw + M_WIDTH], wo[aw + M_WIDTH:]]
    peer = dict(ln2=_row(ln2[i]), wqt=p_wq[i].T.astype(BF16),
                keys=p_keys[i].reshape(2 * P_HEADS, P_NKEYS, P_DKEY // 2).astype(BF16),
                u=p_u[i].astype(BF16), vt=p_v[i].T.astype(BF16))
    return dict(ln1=_row(ln1[i]), in_ws=in_ws, mla_prompt=mla_prompt, mla_sample=mla_sample, wvb_heads=wvb_heads,
                ssd=ssd, gdn=gdn, out_ws=out_ws, peer=peer)


def _rope_tables(pos):
    half = A_ROPE // 2
    inv = ROPE_THETA ** (-jnp.arange(half, dtype=F32) / half)
    ang = pos.astype(F32)[:, None] * inv[None, :]
    cos, sin = jnp.cos(ang), jnp.sin(ang)
    reps = LANES // A_ROPE
    cos_t = jnp.tile(jnp.concatenate([cos, cos], axis=1), (1, reps))
    sin_t = jnp.tile(jnp.concatenate([-sin, sin], axis=1), (1, reps))
    return cos_t, sin_t


def _small_t(small, Bsz, L, c):
    s = small[:, :16].reshape(Bsz, L // c, c, 16)
    return jnp.swapaxes(s, 2, 3)


def _pick_tile(T, cap):
    t = cap
    while T % t:
        t //= 2
    return t


def _token_layer(x, Bsz, L, cos, sin, W, m_buf, m_s, g_buf, g_s, attend, sample):
    T = x.shape[0]
    tm = _pick_tile(T, 256)
    q_lat, kvr, m_z, m_xbc, g_qkv, g_z, small = _in_proj(x, W["ln1"], W["in_ws"], tm)
    mla = _mla_prep(q_lat, kvr, cos, sin, W["mla_sample"] if sample else W["mla_prompt"], tm, sample)
    a_out, cvec, krope, kinv = attend(mla)
    c = CHUNK if L % CHUNK == 0 else L
    smallt = _small_t(small, Bsz, L, c)
    r3 = lambda a: a.reshape(Bsz, L, a.shape[-1])
    m_y, m_buf, m_s = _ssd_scan(r3(m_xbc), r3(m_z), r3(small), smallt, m_buf, m_s, W["ssd"], c)
    g_y, g_buf, g_s = _gdn_scan(r3(g_qkv), r3(g_z), r3(small), smallt, g_buf, g_s, W["gdn"], c,
                                 _pick_tile(Bsz, 4 if sample else 2))
    m_y = m_y.reshape(T, M_WIDTH)
    g_y = g_y.reshape(T, G_WIDTH)
    D = x.shape[1]
    if sample:
        (x,) = _rowwise(_out_proj_sample_body, [a_out, m_y, g_y, x], [W["wvb_heads"]] + W["out_ws"],
                        [(D, F32)], tm, "out_proj_sample")
    else:
        (x,) = _rowwise(_out_proj_prompt_body, [a_out, m_y, g_y, x], W["out_ws"], [(D, F32)], tm, "out_proj_prompt")
    pw = W["peer"]
    h2t, th1, e1, s2, e2 = _peer_query(x, pw["ln2"], pw["wqt"], pw["keys"], tm)
    tp = _pick_tile(T, 512)
    yt = _peer_experts(h2t, th1, e1, s2, e2, pw["u"], pw["vt"], tp, 1024)
    x = _peer_finish(x, yt, tp)
    return x, (cvec, krope, kinv, m_s, m_buf, g_s, g_buf)


def kernel(x_prompt, x_sample, cache_ckv, cache_krope, cache_kscale, state_ssm, state_ssm_conv, state_gdn, state_gdn_conv, page_table, ln1, w_in, w_out, ln2, q_a_norm, w_qb, kv_a_norm, w_kb, w_vb, qn_nope, qn_rope, kn_nope, kn_rope, m_conv_w, m_conv_b, m_dt_bias, m_A_log, m_D, m_norm, g_conv_w, g_dt_bias, g_A_log, g_norm, p_wq, p_keys, p_u, p_v):
    B, S, D = x_prompt.shape
    Bd, Ld, _ = x_sample.shape
    depth = ln1.shape[0]
    n_pages = page_table.shape[1]
    past = n_pages * PAGE_SIZE
    cos_p, sin_p = _rope_tables(jnp.arange(S))
    cos_s, sin_s = _rope_tables(past + jnp.arange(Ld))
    cos_s, sin_s = jnp.tile(cos_s, (Bd, 1)), jnp.tile(sin_s, (Bd, 1))
    krope_t = jnp.swapaxes(cache_krope, -1, -2)
    kscale_t = jnp.swapaxes(cache_kscale, -1, -2)
    npg = _pick_tile(n_pages, 32)
    tq = _pick_tile(S, 512)
    xp = x_prompt.reshape(B * S, D)
    xs = x_sample.reshape(Bd * Ld, D)
    new_p = [[] for _ in range(7)]
    new_s = [[] for _ in range(7)]
    weights = (ln1, w_in, w_out, ln2, q_a_norm, w_qb, kv_a_norm, w_kb, w_vb, qn_nope, qn_rope, kn_nope, kn_rope,
               m_conv_w, m_conv_b, m_dt_bias, m_A_log, m_D, m_norm, g_conv_w, g_dt_bias, g_A_log, g_norm,
               p_wq, p_keys, p_u, p_v)
    for i in range(depth):
        W = _layer_weights(i, *weights)

        def attend_prompt(mla):
            qcat, kcat, v, cvec, krope, kinv = mla
            return _flash_prompt(qcat, kcat, v, B, S, tq), cvec, krope, kinv

        def attend_sample(mla, i=i):
            qa, qr, cvec, krope, kinv = mla
            rows = Ld * A_HEADS
            padk = lambda a: jnp.pad(a.reshape(Bd, Ld, a.shape[-1]), ((0, 0), (0, NEW_PAD - Ld), (0, 0)))
            ksn = jnp.swapaxes(kinv.reshape(Bd, Ld, A_HEADS), 1, 2)
            ksn = jnp.pad(jnp.tile(ksn, (1, Ld, 1)), ((0, 0), (0, 0), (0, NEW_PAD - Ld)))
            o_lat = _decode_attend(page_table, qa.reshape(Bd, rows, KV_LORA), qr.reshape(Bd, rows, A_ROPE),
                                   padk(cvec), padk(krope), ksn, cache_ckv, krope_t, kscale_t, i, npg)
            return o_lat.reshape(Bd * Ld, A_HEADS * KV_LORA), cvec, krope, kinv

        zeros = lambda *s: jnp.zeros(s, F32)
        xp, st_p = _token_layer(xp, B, S, cos_p, sin_p, W, zeros(B, M_CONV - 1, M_CONV_CH),
                                zeros(B, M_HEADS, M_HEADDIM, M_STATE), zeros(B, G_CONV - 1, G_CONV_CH),
                                zeros(B, G_HEADS, G_DK, G_DV), attend_prompt, False)
        xs, st_s = _token_layer(xs, Bd, Ld, cos_s, sin_s, W, state_ssm_conv[i], state_ssm[i],
                                state_gdn_conv[i], state_gdn[i], attend_sample, True)
        for lst, val in zip(new_p, st_p):
            lst.append(val)
        for lst, val in zip(new_s, st_s):
            lst.append(val)

    def pack(vals, Bsz, L):
        cvec, krope, kinv, m_s, m_buf, g_s, g_buf = [jnp.stack(v) for v in vals]
        r = lambda a: a.reshape(depth, Bsz, L, a.shape[-1])
        return r(cvec), r(krope), r(kinv), m_s, m_buf, g_s, g_buf

    return (xp.reshape(B, S, D), xs.reshape(Bd, Ld, D)) + pack(new_p, B, S) + pack(new_s, Bd, Ld)
```

```python
import functools
import math

import jax
import jax.numpy as jnp
import numpy as np
from jax import lax
from jax.experimental import pallas as pl
from jax.experimental.pallas import tpu as pltpu

F32 = jnp.float32
BF16 = jnp.bfloat16
EPS = 1e-6

A_HEADS, A_NOPE, A_ROPE, A_VDIM = 8, 128, 64, 128
Q_LORA, KV_LORA = 384, 256
ROPE_THETA = 10000.0
M_HEADS, M_HEADDIM, M_WIDTH, M_GROUPS, M_STATE, M_CONV = 8, 64, 512, 2, 128, 4
M_CONV_CH = M_WIDTH + 2 * M_GROUPS * M_STATE
G_HEADS, G_DK, G_DV, G_WIDTH, G_CONV = 4, 128, 128, 512, 4
G_CONV_CH = 2 * G_HEADS * G_DK + G_WIDTH
IN_SIZES = (Q_LORA, KV_LORA, A_ROPE, M_WIDTH, M_CONV_CH, M_HEADS, G_CONV_CH, G_WIDTH, G_HEADS, G_HEADS)
P_HEADS, P_NKEYS, P_DKEY, P_TOPK = 8, 128, 256, 16
PAGE_SIZE = 128
CHUNK = 64
LANES = 128
SMALL_W = LANES
G_B_COL, G_A_COL = M_HEADS, M_HEADS + G_HEADS
VMEM_LIMIT = 56 * 1024 * 1024
ATT_SCALE = (A_NOPE + A_ROPE) ** -0.5


def _mm(a, b):
    return jnp.dot(a.astype(BF16), b.astype(BF16), preferred_element_type=F32)


def _mm_nt(a, b):
    return lax.dot_general(a.astype(BF16), b.astype(BF16), (((1,), (1,)), ((), ())),
                           preferred_element_type=F32)


def _mm_tn(a, b):
    return lax.dot_general(a.astype(BF16), b.astype(BF16), (((0,), (0,)), ((), ())),
                           preferred_element_type=F32)


def _hdot(a, b):
    return jnp.dot(a, b, precision=lax.Precision.HIGHEST, preferred_element_type=F32)


def _split3(x):
    hi = x.astype(BF16)
    r = x - hi.astype(F32)
    mid = r.astype(BF16)
    lo = (r - mid.astype(F32)).astype(BF16)
    return hi, mid, lo


def _dot_exact_left(sel, x):
    sb = sel.astype(BF16)
    return sum(jnp.dot(sb, p, preferred_element_type=F32) for p in _split3(x))


def _dot_exact_right(x, sel):
    sb = sel.astype(BF16)
    return sum(jnp.dot(p, sb, preferred_element_type=F32) for p in _split3(x))


def _dot3(a, b):
    ah = a.astype(BF16)
    al = (a - ah.astype(F32)).astype(BF16)
    bh = b.astype(BF16)
    bl = (b - bh.astype(F32)).astype(BF16)
    return (jnp.dot(ah, bh, preferred_element_type=F32) + jnp.dot(ah, bl, preferred_element_type=F32)
            + jnp.dot(al, bh, preferred_element_type=F32))


def _rms(x, g):
    return x * lax.rsqrt(jnp.mean(x * x, axis=-1, keepdims=True) + EPS) * g


def _softplus(x):
    return jnp.maximum(x, 0.0) + jnp.log1p(jnp.exp(-jnp.abs(x)))


def _silu(x):
    return x * jax.nn.sigmoid(x)


def _const_spec(a):
    nd = a.ndim
    return pl.BlockSpec(a.shape, lambda *_: (0,) * nd, pipeline_mode=pl.Buffered(1))


def _params(sem):
    return pltpu.CompilerParams(dimension_semantics=sem, vmem_limit_bytes=VMEM_LIMIT)


def _rowwise(body, rows, consts, outs, tm, name):
    T = rows[0].shape[0]
    assert T % tm == 0
    nr, nc = len(rows), len(consts)

    def kern(*refs):
        body(refs[:nr], refs[nr:nr + nc], refs[nr + nc:])

    in_specs = []
    for a in rows:
        assert a.shape[0] % tm == 0
        per = a.shape[0] // tm
        if a.shape[0] == T:
            in_specs.append(pl.BlockSpec((tm, a.shape[1]), lambda i: (i, 0)))
        else:
            in_specs.append(pl.BlockSpec((tm, a.shape[1]), lambda i, per=per: (i % per, 0)))
    in_specs += [_const_spec(a) for a in consts]
    out_specs = [pl.BlockSpec((tm, c), lambda i: (i, 0)) for c, _ in outs]
    out_shape = [jax.ShapeDtypeStruct((T, c), dt) for c, dt in outs]
    return pl.pallas_call(kern, grid=(T // tm,), in_specs=in_specs, out_specs=out_specs,
                          out_shape=out_shape, compiler_params=_params(("parallel",)),
                          name=name)(*rows, *consts)


def _in_proj_body(rows, consts, outs):
    x = rows[0][...]
    h = _rms(x, consts[0][...]).astype(BF16)
    for w_ref, o_ref in zip(consts[1:], outs):
        o_ref[...] = jnp.dot(h, w_ref[...], preferred_element_type=F32)


def _in_proj(x, ln1, ws, tm):
    outs = [(w.shape[1], F32) for w in ws]
    return _rowwise(_in_proj_body, [x], [ln1] + list(ws), outs, tm, "in_proj")


def _rope_rot(x, cos, sin_signed):
    w = x.shape[-1]
    lane = lax.broadcasted_iota(jnp.int32, x.shape, 1)
    first = (lane & (A_ROPE - 1)) < (A_ROPE // 2)
    rolled = jnp.where(first, pltpu.roll(x, w - A_ROPE // 2, 1), pltpu.roll(x, A_ROPE // 2, 1))
    return x * cos + rolled * sin_signed


def _mla_common(q_lat, kvr, cos, sin, qan, w_qn, w_qr, qn_rope, kvan, kn_rope, w_kb):
    ql = _rms(q_lat, qan).astype(BF16)
    qn = jnp.dot(ql, w_qn, preferred_element_type=F32)
    qr = jnp.dot(ql, w_qr, preferred_element_type=F32)
    wq = qr.shape[1]
    r = lax.broadcasted_iota(jnp.int32, (wq, wq), 0) >> 6
    c = lax.broadcasted_iota(jnp.int32, (wq, wq), 1) >> 6
    head_ones = (r == c).astype(F32)
    ss = _hdot(qr * qr, head_ones)
    qr = qr * lax.rsqrt(ss * (1.0 / A_ROPE) + EPS) * qn_rope
    reps = wq // LANES
    qr = _rope_rot(qr, jnp.tile(cos, (1, reps)), jnp.tile(sin, (1, reps)))
    kv = kvr[:, :KV_LORA]
    kr = kvr[:, KV_LORA:]
    cvec = _rms(kv, kvan)
    krn = kr * lax.rsqrt(jnp.sum(kr * kr, axis=-1, keepdims=True) * (1.0 / A_ROPE) + EPS) * kn_rope
    krope = _rope_rot(krn, cos, sin)
    kraw = jnp.dot(cvec.astype(BF16), w_kb, preferred_element_type=F32)
    return qn, qr, cvec, krope, kraw


def _k_inv_heads(kraw):
    tm = kraw.shape[0]
    lane8 = lax.broadcasted_iota(jnp.int32, (tm, A_HEADS), 1)
    kinv8 = jnp.zeros((tm, A_HEADS), F32)
    invs = []
    for h in range(A_HEADS):
        blk = kraw[:, h * A_NOPE:(h + 1) * A_NOPE]
        inv = lax.rsqrt(jnp.mean(blk * blk, axis=-1, keepdims=True) + EPS)
        invs.append(inv)
        kinv8 = jnp.where(lane8 == h, inv, kinv8)
    return invs, kinv8


def _mla_prompt_kernel(qlat_ref, kvr_ref, cos_ref, sin_ref, qan_ref, wqn_ref, wqr_ref, qnn_ref, qnr_ref,
                       kvan_ref, knr_ref, wkb_ref, knn_ref, wvb_ref,
                       qcat_ref, kcat_ref, v_ref, c_ref, krope_ref, kinv_ref):
    qn, qr, cvec, krope, kraw = _mla_common(
        qlat_ref[...], kvr_ref[...], cos_ref[...], sin_ref[...], qan_ref[...], wqn_ref[...], wqr_ref[...],
        qnr_ref[...], kvan_ref[...], knr_ref[...], wkb_ref[...])
    invs, kinv8 = _k_inv_heads(kraw)
    kr64 = krope[:, :A_ROPE]
    for h in range(A_HEADS):
        qn_h = _rms(qn[:, h * A_NOPE:(h + 1) * A_NOPE], qnn_ref[...])
        qcat_ref[h] = jnp.concatenate([qn_h, qr[:, h * A_ROPE:(h + 1) * A_ROPE]], axis=-1).astype(BF16)
        kn_h = kraw[:, h * A_NOPE:(h + 1) * A_NOPE] * invs[h] * knn_ref[...]
        kcat_ref[h] = jnp.concatenate([kn_h, kr64], axis=-1).astype(BF16)
    v_ref[...] = jnp.dot(cvec.astype(BF16), wvb_ref[...], preferred_element_type=F32).astype(BF16)
    c_ref[...] = cvec
    krope_ref[...] = kr64
    kinv_ref[...] = kinv8


def _mla_sample_kernel(qlat_ref, kvr_ref, cos_ref, sin_ref, qan_ref, wqn_ref, wqr_ref, qnn_ref, qnr_ref,
                       kvan_ref, knr_ref, wkb_ref, knn_ref, wkbt_ref,
                       qa_ref, qr_ref, c_ref, krope_ref, kinv_ref):
    qn, qr, cvec, krope, kraw = _mla_common(
        qlat_ref[...], kvr_ref[...], cos_ref[...], sin_ref[...], qan_ref[...], wqn_ref[...], wqr_ref[...],
        qnr_ref[...], kvan_ref[...], knr_ref[...], wkb_ref[...])
    _, kinv8 = _k_inv_heads(kraw)
    for h in range(A_HEADS):
        qn_h = _rms(qn[:, h * A_NOPE:(h + 1) * A_NOPE], qnn_ref[...]) * knn_ref[...]
        qa_ref[:, h * KV_LORA:(h + 1) * KV_LORA] = jnp.dot(
            qn_h.astype(BF16), wkbt_ref[h], preferred_element_type=F32).astype(BF16)
    qr_ref[...] = qr.astype(BF16)
    c_ref[...] = cvec
    krope_ref[...] = krope[:, :A_ROPE]
    kinv_ref[...] = kinv8


def _mla_prep(q_lat, kvr, cos, sin, consts, tm, sample):
    T = q_lat.shape[0]
    rows = [q_lat, kvr, cos, sin]
    in_specs = []
    for a in rows:
        per = a.shape[0] // tm
        if a.shape[0] == T:
            in_specs.append(pl.BlockSpec((tm, a.shape[1]), lambda i: (i, 0)))
        else:
            in_specs.append(pl.BlockSpec((tm, a.shape[1]), lambda i, per=per: (i % per, 0)))
    in_specs += [_const_spec(a) for a in consts]
    row_out = lambda c: pl.BlockSpec((tm, c), lambda i: (i, 0))
    tail_shapes = [jax.ShapeDtypeStruct((T, KV_LORA), F32), jax.ShapeDtypeStruct((T, A_ROPE), F32),
                   jax.ShapeDtypeStruct((T, A_HEADS), F32)]
    tail_specs = [row_out(KV_LORA), row_out(A_ROPE), row_out(A_HEADS)]
    if sample:
        kern = _mla_sample_kernel
        out_shape = [jax.ShapeDtypeStruct((T, A_HEADS * KV_LORA), BF16),
                     jax.ShapeDtypeStruct((T, A_HEADS * A_ROPE), BF16)] + tail_shapes
        out_specs = [row_out(A_HEADS * KV_LORA), row_out(A_HEADS * A_ROPE)] + tail_specs
    else:
        kern = _mla_prompt_kernel
        dqk = A_NOPE + A_ROPE
        head_spec = pl.BlockSpec((A_HEADS, tm, dqk), lambda i: (0, i, 0))
        out_shape = [jax.ShapeDtypeStruct((A_HEADS, T, dqk), BF16), jax.ShapeDtypeStruct((A_HEADS, T, dqk), BF16),
                     jax.ShapeDtypeStruct((T, A_HEADS * A_VDIM), BF16)] + tail_shapes
        out_specs = [head_spec, head_spec, row_out(A_HEADS * A_VDIM)] + tail_specs
    return pl.pallas_call(kern, grid=(T // tm,), in_specs=in_specs, out_specs=out_specs, out_shape=out_shape,
                          compiler_params=_params(("parallel",)),
                          name="mla_prep_sample" if sample else "mla_prep_prompt")(*rows, *consts)


def _flash_kernel(q_ref, k_ref, v_ref, o_ref, *, tq):
    i = pl.program_id(2)
    q = q_ref[0]
    row = lax.broadcasted_iota(jnp.int32, (tq, tq), 0)
    col = lax.broadcasted_iota(jnp.int32, (tq, tq), 1)

    def step(j, carry, diagonal):
        m, l, acc = carry
        start = pl.multiple_of(j * tq, tq)
        k = k_ref[0, pl.ds(start, tq), :]
        v = v_ref[pl.ds(start, tq), :]
        s = lax.dot_general(q, k, (((1,), (1,)), ((), ())), preferred_element_type=F32) * ATT_SCALE
        if diagonal:
            s = jnp.where(col <= row, s, -jnp.inf)
        m_new = jnp.maximum(m, jnp.max(s, axis=-1, keepdims=True))
        alpha = jnp.exp(m - m_new)
        p = jnp.exp(s - m_new)
        l = alpha * l + jnp.sum(p, axis=-1, keepdims=True)
        acc = alpha * acc + jnp.dot(p.astype(BF16), v, preferred_element_type=F32)
        return m_new, l, acc

    init = (jnp.full((tq, 1), -jnp.inf, F32), jnp.zeros((tq, 1), F32), jnp.zeros((tq, A_VDIM), F32))
    carry = lax.fori_loop(0, i, lambda j, c: step(j, c, False), init)
    _, l, acc = step(i, carry, True)
    o_ref[...] = (acc / l).astype(o_ref.dtype)


def _flash_prompt(qcat, kcat, v, Bsz, S, tq):
    nq = S // tq
    dqk = A_NOPE + A_ROPE
    T = Bsz * S
    return pl.pallas_call(
        functools.partial(_flash_kernel, tq=tq),
        grid=(Bsz, A_HEADS, nq),
        in_specs=[pl.BlockSpec((1, tq, dqk), lambda b, h, i: (h, b * nq + i, 0)),
                  pl.BlockSpec((1, S, dqk), lambda b, h, i: (h, b, 0)),
                  pl.BlockSpec((S, A_VDIM), lambda b, h, i: (b, h))],
        out_specs=pl.BlockSpec((tq, A_VDIM), lambda b, h, i: (b * nq + i, h)),
        out_shape=jax.ShapeDtypeStruct((T, A_HEADS * A_VDIM), BF16),
        compiler_params=_params(("parallel", "parallel", "arbitrary")),
        name="flash_prompt")(qcat, kcat, v)


NEW_PAD = 16


def _decode_kernel(pt_ref, qa_ref, qr_ref, cn_ref, krn_ref, ksn_ref, ckv_hbm, krt_hbm, kst_hbm, o_ref,
                   ck_buf, kr_buf, ks_buf, sem, *, li, npg):
    b = pl.program_id(0)
    nseq = pl.num_programs(0)
    rows = qa_ref.shape[1]
    reps = rows // A_HEADS
    ngroups = pt_ref.shape[1] // npg

    def group_copies(seq, g, slot):
        cps = []
        for p in range(npg):
            pid = pt_ref[seq, g * npg + p]
            cps.append(pltpu.make_async_copy(ckv_hbm.at[li, pid], ck_buf.at[slot, p], sem.at[slot, 0]))
            cps.append(pltpu.make_async_copy(krt_hbm.at[li, pid], kr_buf.at[slot, p], sem.at[slot, 1]))
            cps.append(pltpu.make_async_copy(kst_hbm.at[li, pid], ks_buf.at[slot, p], sem.at[slot, 2]))
        return cps

    @pl.when(b == 0)
    def _():
        for cp in group_copies(0, 0, 0):
            cp.start()

    qa = qa_ref[0]
    qr = qr_ref[0]
    nt = (((1,), (1,)), ((), ()))

    def body(g, carry):
        m_old, l_old, acc_old = carry
        gg = b * ngroups + g
        slot = gg & 1
        last = g + 1 == ngroups
        nxt_seq = jnp.where(last, b + 1, b)
        nxt_g = jnp.where(last, 0, g + 1)

        @pl.when(gg + 1 < nseq * ngroups)
        def _():
            for cp in group_copies(nxt_seq, nxt_g, 1 - slot):
                cp.start()

        for cp in group_copies(b, g, slot):
            cp.wait()
        m_run, l_run, acc = m_old, l_old, acc_old
        nsplit = 2 if npg % 2 == 0 else 1
        per = npg // nsplit
        for part in range(nsplit):
            ss, ccs = [], []
            for p in range(part * per, (part + 1) * per):
                cc = ck_buf[slot, p].astype(BF16)
                s = lax.dot_general(qa, cc, nt, preferred_element_type=F32)
                s = s * jnp.tile(ks_buf[slot, p], (reps, 1))
                s = s + jnp.dot(qr, kr_buf[slot, p].astype(BF16), preferred_element_type=F32)
                ss.append(s * ATT_SCALE)
                ccs.append(cc)
            smax = ss[0]
            for s in ss[1:]:
                smax = jnp.maximum(smax, s)
            m_new = jnp.maximum(m_run, jnp.max(smax, axis=-1, keepdims=True))
            alpha = jnp.exp(m_run - m_new)
            acc = alpha * acc
            esum = None
            for s, cc in zip(ss, ccs):
                e = jnp.exp(s - m_new)
                esum = e if esum is None else esum + e
                acc = acc + jnp.dot(e.astype(BF16), cc, preferred_element_type=F32)
            l_run = alpha * l_run + jnp.sum(esum, axis=-1, keepdims=True)
            m_run = m_new
        return m_run, l_run, acc

    init = (jnp.full((rows, 1), -jnp.inf, F32), jnp.zeros((rows, 1), F32), jnp.zeros((rows, KV_LORA), F32))
    m1, l1, acc1 = lax.fori_loop(0, ngroups, body, init)

    cn = cn_ref[0].astype(BF16)
    s = lax.dot_general(qa, cn, nt, preferred_element_type=F32) * ksn_ref[0]
    s = s + lax.dot_general(qr, krn_ref[0].astype(BF16), nt, preferred_element_type=F32)
    row = lax.broadcasted_iota(jnp.int32, s.shape, 0)
    col = lax.broadcasted_iota(jnp.int32, s.shape, 1)
    s = jnp.where(col <= (row >> 3), s * ATT_SCALE, -jnp.inf)
    m2 = jnp.maximum(m1, jnp.max(s, axis=-1, keepdims=True))
    a2 = jnp.exp(m1 - m2)
    e = jnp.exp(s - m2)
    l2 = a2 * l1 + jnp.sum(e, axis=-1, keepdims=True)
    acc2 = a2 * acc1 + jnp.dot(e.astype(BF16), cn, preferred_element_type=F32)
    o_ref[0] = acc2 / l2


def _decode_attend(page_table, qa, qr, cn, krn, ksn, cache_ckv, cache_krope_t, cache_kscale_t, li, npg):
    Bd, n_pages = page_table.shape
    rows = qa.shape[1]
    assert n_pages % npg == 0
    per_b = lambda b, pt: (b, 0, 0)
    hbm = pl.BlockSpec(memory_space=pl.ANY)
    in_specs = [pl.BlockSpec((1, rows, KV_LORA), per_b), pl.BlockSpec((1, rows, A_ROPE), per_b),
                pl.BlockSpec((1, NEW_PAD, KV_LORA), per_b), pl.BlockSpec((1, NEW_PAD, A_ROPE), per_b),
                pl.BlockSpec((1, rows, NEW_PAD), per_b), hbm, hbm, hbm]
    grid_spec = pltpu.PrefetchScalarGridSpec(
        num_scalar_prefetch=1, grid=(Bd,), in_specs=in_specs,
        out_specs=pl.BlockSpec((1, rows, KV_LORA), per_b),
        scratch_shapes=[pltpu.VMEM((2, npg, PAGE_SIZE, KV_LORA), F32), pltpu.VMEM((2, npg, A_ROPE, PAGE_SIZE), F32),
                        pltpu.VMEM((2, npg, A_HEADS, PAGE_SIZE), F32), pltpu.SemaphoreType.DMA((2, 3))])
    return pl.pallas_call(
        functools.partial(_decode_kernel, li=li, npg=npg), grid_spec=grid_spec,
        out_shape=jax.ShapeDtypeStruct((Bd, rows, KV_LORA), F32),
        compiler_params=_params(("arbitrary",)), name="decode_attend")(
            page_table, qa, qr, cn, krn, ksn, cache_ckv, cache_krope_t, cache_kscale_t)


def _tri_masks(c):
    r = lax.broadcasted_iota(jnp.int32, (c, c), 0)
    col = lax.broadcasted_iota(jnp.int32, (c, c), 1)
    return r, col


def _conv_silu(x_ref, buf0_ref, bufout_ref, xe_ref, w_ref, bias, j, c, b=0):
    @pl.when(j == 0)
    def _():
        xe_ref[b, 5:8, :] = buf0_ref[b]

    xe_ref[b, 8:8 + c, :] = x_ref[b]
    y = xe_ref[b, pl.ds(5, c), :] * w_ref[0:1, :]
    for k in range(1, 4):
        y = y + xe_ref[b, pl.ds(5 + k, c), :] * w_ref[k:k + 1, :]
    if bias is not None:
        y = y + bias
    bufout_ref[b] = xe_ref[b, c + 5:c + 8, :]
    tail = xe_ref[b, c:c + 8, :]
    xe_ref[b, 0:8, :] = tail
    return _silu(y)


def _ssd_kernel(xbc_ref, z_ref, small_ref, smallt_ref, buf0_ref, s0_ref, cw_ref, cb_ref, dtb_ref, alog_ref,
                dtbc_ref, alogc_ref, dskip_ref, ng_ref, y_ref, bufout_ref, sout_ref, xe_ref, *, c, nb):
    j = pl.program_id(1)

    @pl.when(j == 0)
    def _():
        sout_ref[...] = s0_ref[...]

    gw = M_GROUPS * M_STATE
    rep = M_HEADS // M_GROUPS
    r, col = _tri_masks(c)
    tri = col <= r
    xs, bm, cm, dt, dtt, cum, cumt, cb = [], [], [], [], [], [], [], []
    for b in range(nb):
        xbc = _conv_silu(xbc_ref, buf0_ref, bufout_ref, xe_ref, cw_ref, cb_ref[...], j, c, b)
        xs.append(xbc[:, :M_WIDTH])
        bm.append(xbc[:, M_WIDTH:M_WIDTH + gw])
        cm.append(xbc[:, M_WIDTH + gw:])
        dt.append(_softplus(small_ref[b] + dtb_ref[...]))
        dtt.append(_softplus(smallt_ref[b, 0][:M_HEADS] + dtbc_ref[...]))
        cum.append(_dot_exact_left(tri, dt[b] * (-jnp.exp(alog_ref[...]))))
        cumt.append(_dot_exact_right(dtt[b] * (-jnp.exp(alogc_ref[...])), r <= col))
        cb.append([_mm_nt(cm[b][:, g * M_STATE:(g + 1) * M_STATE], bm[b][:, g * M_STATE:(g + 1) * M_STATE])
                   for g in range(M_GROUPS)])
    chains = [(b, h) for b in range(nb) for h in range(M_HEADS)]
    grp = lambda t, b, h: t[b][:, (h // rep) * M_STATE:(h // rep + 1) * M_STATE]
    cc = [cum[b][:, h:h + 1] for b, h in chains]
    mmat = [cb[b][h // rep] * jnp.exp(jnp.where(tri, cc[i] - cumt[b][h:h + 1, :], -jnp.inf)) * dtt[b][h:h + 1, :]
            for i, (b, h) in enumerate(chains)]
    xh = [xs[b][:, h * M_HEADDIM:(h + 1) * M_HEADDIM] for b, h in chains]
    s_old = [sout_ref[b, h] for b, h in chains]
    y = [_mm(mmat[i], xh[i]) + _mm_nt(grp(cm, b, h), s_old[i]) * jnp.exp(cc[i])
         for i, (b, h) in enumerate(chains)]
    for i, (b, h) in enumerate(chains):
        clast = cum[b][c - 1:c, h:h + 1]
        to_end = jnp.exp(clast - cc[i]) * dt[b][:, h:h + 1]
        sout_ref[b, h] = s_old[i] * jnp.exp(clast) + _mm_tn(xh[i] * to_end, grp(bm, b, h))
    for b in range(nb):
        yb = jnp.concatenate(y[b * M_HEADS:(b + 1) * M_HEADS], axis=-1) + xs[b] * dskip_ref[...]
        y_ref[b] = _rms(yb * _silu(z_ref[b]), ng_ref[...])


def _ssd_scan(xbc, z, small, smallt, buf0, s0, consts, c, nb):
    Bsz, L, _ = xbc.shape
    nc = L // c
    assert Bsz % nb == 0
    blk = lambda w: pl.BlockSpec((nb, c, w), lambda b, j: (b, j, 0))
    buf_spec = pl.BlockSpec((nb, M_CONV - 1, M_CONV_CH), lambda b, j: (b, 0, 0))
    state_spec = pl.BlockSpec((nb, M_HEADS, M_HEADDIM, M_STATE), lambda b, j: (b, 0, 0, 0))
    in_specs = [blk(M_CONV_CH), blk(M_WIDTH), blk(SMALL_W),
                pl.BlockSpec((nb, 1, 16, c), lambda b, j: (b, j, 0, 0)), buf_spec, state_spec]
    in_specs += [_const_spec(a) for a in consts]
    out_shape = [jax.ShapeDtypeStruct((Bsz, L, M_WIDTH), F32),
                 jax.ShapeDtypeStruct((Bsz, M_CONV - 1, M_CONV_CH), F32),
                 jax.ShapeDtypeStruct((Bsz, M_HEADS, M_HEADDIM, M_STATE), F32)]
    return pl.pallas_call(
        functools.partial(_ssd_kernel, c=c, nb=nb), grid=(Bsz // nb, nc), in_specs=in_specs,
        out_specs=[blk(M_WIDTH), buf_spec, state_spec],
        out_shape=out_shape, scratch_shapes=[pltpu.VMEM((nb, c + 8, M_CONV_CH), F32)],
        compiler_params=_params(("parallel", "arbitrary")), name="ssd_scan")(
            xbc, z, small, smallt, buf0, s0, *consts)


def _unit_lower_inv_many(amats, c, r, col):
    eye = (r == col).astype(F32)
    same8 = (r >> 3) == (col >> 3)
    a8 = [jnp.where(same8, a, 0.0) for a in amats]
    x = [eye - a for a in a8]
    p = [_dot3(a, a) for a in a8]
    x = [xi + _dot3(xi, pi) for xi, pi in zip(x, p)]
    p = [_dot3(pi, pi) for pi in p]
    x = [xi + _dot3(xi, pi) for xi, pi in zip(x, p)]
    b = 8
    while b < c:
        sh = int(math.log2(b))
        pick = ((r >> (sh + 1)) == (col >> (sh + 1))) & ((r >> sh) != (col >> sh))
        low = [jnp.where(pick, a, 0.0) for a in amats]
        t = [_dot3(xi, li) for xi, li in zip(x, low)]
        x = [xi - _dot3(ti, xi) for xi, ti in zip(x, t)]
        b *= 2
    return x


def _gdn_kernel(qkv_ref, z_ref, small_ref, smallt_ref, buf0_ref, s0_ref, cw_ref, gdt_ref, galog_ref,
                gdtc_ref, galogc_ref, ng_ref, o_ref, bufout_ref, sout_ref, xe_ref, *, c, nb):
    j = pl.program_id(1)

    @pl.when(j == 0)
    def _():
        sout_ref[...] = s0_ref[...]

    nq = G_HEADS * G_DK
    r, col = _tri_masks(c)
    incl = col <= r
    strict = col < r
    chains = [(b, h) for b in range(nb) for h in range(G_HEADS)]
    qkv, small, gcum, gcumt = [], [], [], []
    for b in range(nb):
        qkv.append(_conv_silu(qkv_ref, buf0_ref, bufout_ref, xe_ref, cw_ref, None, j, c, b))
        small.append(small_ref[b])
        gall = -jnp.exp(galog_ref[...]) * _softplus(small[b] + gdt_ref[...])
        gallt = -jnp.exp(galogc_ref[...]) * _softplus(smallt_ref[b, 0] + gdtc_ref[...])
        gcum.append(_dot_exact_left(incl, gall))
        gcumt.append(_dot_exact_right(gallt, r <= col))
    q, k, v, beta, gc, decay = [], [], [], [], [], []
    for b, h in chains:
        qh = qkv[b][:, h * G_DK:(h + 1) * G_DK]
        kh = qkv[b][:, nq + h * G_DK:nq + (h + 1) * G_DK]
        q.append(qh * lax.rsqrt(jnp.sum(qh * qh, axis=-1, keepdims=True) + EPS) * (G_DK ** -0.5))
        k.append(kh * lax.rsqrt(jnp.sum(kh * kh, axis=-1, keepdims=True) + EPS))
        v.append(qkv[b][:, 2 * nq + h * G_DV:2 * nq + (h + 1) * G_DV])
        beta.append(jax.nn.sigmoid(small[b][:, G_B_COL + h:G_B_COL + h + 1]))
        gc.append(gcum[b][:, G_A_COL + h:G_A_COL + h + 1])
        gr = gcumt[b][G_A_COL + h:G_A_COL + h + 1, :]
        decay.append(jnp.exp(jnp.where(incl, gc[-1] - gr, -jnp.inf)))
    n = len(chains)
    kb = [k[i] * beta[i] for i in range(n)]
    amat = [jnp.where(strict, _mm_nt(kb[i], k[i]) * decay[i], 0.0) for i in range(n)]
    tmat = _unit_lower_inv_many(amat, c, r, col)
    u = [_mm(tmat[i], v[i] * beta[i]) for i in range(n)]
    w = [_mm(tmat[i], kb[i] * jnp.exp(gc[i])) for i in range(n)]
    attn = [jnp.where(incl, _mm_nt(q[i], k[i]) * decay[i], 0.0) for i in range(n)]
    s_old = [sout_ref[b, h] for b, h in chains]
    v_new = [u[i] - _mm(w[i], s_old[i]) for i in range(n)]
    o = [_mm(q[i] * jnp.exp(gc[i]), s_old[i]) + _mm(attn[i], v_new[i]) for i in range(n)]
    for i, (b, h) in enumerate(chains):
        glast = gcum[b][c - 1:c, G_A_COL + h:G_A_COL + h + 1]
        sout_ref[b, h] = s_old[i] * jnp.exp(glast) + _mm_tn(k[i] * jnp.exp(glast - gc[i]), v_new[i])
    for b in range(nb):
        z = z_ref[b]
        outs = [_rms(o[b * G_HEADS + h], ng_ref[...]) * _silu(z[:, h * G_DV:(h + 1) * G_DV])
                for h in range(G_HEADS)]
        o_ref[b] = jnp.concatenate(outs, axis=-1)


def _gdn_scan(qkv, z, small, smallt, buf0, s0, consts, c, nb):
    Bsz, L, _ = qkv.shape
    nc = L // c
    assert Bsz % nb == 0
    blk = lambda w: pl.BlockSpec((nb, c, w), lambda b, j: (b, j, 0))
    buf_spec = pl.BlockSpec((nb, G_CONV - 1, G_CONV_CH), lambda b, j: (b, 0, 0))
    state_spec = pl.BlockSpec((nb, G_HEADS, G_DK, G_DV), lambda b, j: (b, 0, 0, 0))
    in_specs = [blk(G_CONV_CH), blk(G_WIDTH), blk(SMALL_W),
                pl.BlockSpec((nb, 1, 16, c), lambda b, j: (b, j, 0, 0)), buf_spec, state_spec]
    in_specs += [_const_spec(a) for a in consts]
    out_shape = [jax.ShapeDtypeStruct((Bsz, L, G_WIDTH), F32),
                 jax.ShapeDtypeStruct((Bsz, G_CONV - 1, G_CONV_CH), F32),
                 jax.ShapeDtypeStruct((Bsz, G_HEADS, G_DK, G_DV), F32)]
    return pl.pallas_call(
        functools.partial(_gdn_kernel, c=c, nb=nb), grid=(Bsz // nb, nc), in_specs=in_specs,
        out_specs=[blk(G_WIDTH), buf_spec, state_spec],
        out_shape=out_shape, scratch_shapes=[pltpu.VMEM((nb, c + 8, G_CONV_CH), F32)],
        compiler_params=_params(("parallel", "arbitrary")), name="gdn_scan")(
            qkv, z, small, smallt, buf0, s0, *consts)


def _out_proj_prompt_body(rows, consts, outs):
    a, m, g, x = rows
    wa, wm, wg = consts
    outs[0][...] = x[...] + _mm(a[...], wa[...]) + _mm(m[...], wm[...]) + _mm(g[...], wg[...])


def _out_proj_sample_body(rows, consts, outs):
    olat, m, g, x = rows
    wvb, wa, wm, wg = consts
    heads = [_mm(olat[:, h * KV_LORA:(h + 1) * KV_LORA], wvb[h]) for h in range(A_HEADS)]
    a = jnp.concatenate(heads, axis=-1)
    outs[0][...] = x[...] + _mm(a, wa[...]) + _mm(m[...], wm[...]) + _mm(g[...], wg[...])


def _top_values(s, k):
    out = []
    for it in range(k):
        m = jnp.max(s, axis=0, keepdims=True)
        out.append(m)
        if it + 1 < k:
            s = jnp.where(s == m, -jnp.inf, s)
    return out


def _peer_query_kernel(x_ref, ln2_ref, wqt_ref, keys_ref, h2t_ref, th1_ref, e1_ref, s2_ref, e2_ref):
    h2t = _rms(x_ref[...], ln2_ref[...]).T.astype(BF16)
    h2t_ref[...] = h2t
    qt = jnp.dot(wqt_ref[...], h2t, preferred_element_type=F32).astype(BF16)
    half = P_DKEY // 2
    for h in range(P_HEADS):
        s1 = jnp.dot(keys_ref[2 * h], qt[(2 * h) * half:(2 * h + 1) * half], preferred_element_type=F32)
        s2 = jnp.dot(keys_ref[2 * h + 1], qt[(2 * h + 1) * half:(2 * h + 2) * half], preferred_element_type=F32)
        v1 = _top_values(s1, P_TOPK)
        v2 = _top_values(s2, P_TOPK)
        v1m = jnp.concatenate(v1, axis=0)
        v2m = jnp.concatenate(v2, axis=0)
        hk = P_TOPK // 2
        cand = jnp.concatenate([v1[0] + v2m] + [v1[a] + v2m[:hk] for a in range(1, hk)] + [v1m[hk:] + v2[0]],
                               axis=0)
        best = _top_values(cand, P_TOPK)
        zsum = jnp.ones_like(best[0])
        for b in best[1:]:
            zsum = zsum + jnp.exp(b - best[0])
        th1_ref[h] = best[P_TOPK - 1] - s1
        s2_ref[h] = s2
        e1_ref[h] = jnp.exp(s1 - v1[0])
        e2_ref[h] = jnp.exp(s2 - v2[0]) * (0.5 / zsum)


def _peer_query(x, ln2, wqt, keys, tm):
    T, D = x.shape
    hk = pl.BlockSpec((P_HEADS, P_NKEYS, tm), lambda i: (0, 0, i))
    hk_shape = jax.ShapeDtypeStruct((P_HEADS, P_NKEYS, T), F32)
    return pl.pallas_call(
        _peer_query_kernel, grid=(T // tm,),
        in_specs=[pl.BlockSpec((tm, D), lambda i: (i, 0)), _const_spec(ln2), _const_spec(wqt), _const_spec(keys)],
        out_specs=[pl.BlockSpec((D, tm), lambda i: (0, i)), hk, hk, hk, hk],
        out_shape=[jax.ShapeDtypeStruct((D, T), BF16), hk_shape, hk_shape, hk_shape, hk_shape],
        compiler_params=_params(("parallel",)), name="peer_query")(x, ln2, wqt, keys)


def _gelu_x2(x):
    return x * (1.0 + lax.erf(x * (2.0 ** -0.5)))


def _peer_gate(th1_ref, e1_ref, s2_ref, e2_ref, i1):
    g = None
    for h in range(P_HEADS):
        w = jnp.where(s2_ref[h] >= th1_ref[h, pl.ds(i1, 1), :], e2_ref[h] * e1_ref[h, pl.ds(i1, 1), :], 0.0)
        g = w if g is None else g + w
    return g


def _peer_expert_kernel(h2t_ref, th1_ref, e1_ref, s2_ref, e2_ref, u_ref, vt_ref, y_ref, ga_ref, gb_ref, *, ec):
    j = pl.program_id(1)
    nj = pl.num_programs(1)
    half = ec // 2
    nsub = half // P_NKEYS

    def fill(g_ref, i1_base):
        for r in range(nsub):
            g_ref[r * P_NKEYS:(r + 1) * P_NKEYS, :] = _peer_gate(th1_ref, e1_ref, s2_ref, e2_ref, i1_base + r)

    @pl.when(j == 0)
    def _():
        fill(ga_ref, 0)
        y_ref[...] = jnp.zeros(y_ref.shape, F32)

    pt_a = jnp.dot(u_ref[:half, :], h2t_ref[...], preferred_element_type=F32)
    fill(gb_ref, j * 2 * nsub + nsub)
    act_a = (_gelu_x2(pt_a) * ga_ref[...]).astype(BF16)
    pt_b = jnp.dot(u_ref[half:, :], h2t_ref[...], preferred_element_type=F32)
    fill(ga_ref, jnp.minimum(j + 1, nj - 1) * 2 * nsub)
    act_b = (_gelu_x2(pt_b) * gb_ref[...]).astype(BF16)
    act = jnp.concatenate([act_a, act_b], axis=0)
    y_ref[...] += jnp.dot(vt_ref[...], act, preferred_element_type=F32)


def _peer_experts(h2t, th1, e1, s2, e2, u, vt, tm, ec):
    D, T = h2t.shape
    ne = u.shape[0]
    hk = pl.BlockSpec((P_HEADS, P_NKEYS, tm), lambda i, j: (0, 0, i))
    return pl.pallas_call(
        functools.partial(_peer_expert_kernel, ec=ec), grid=(T // tm, ne // ec),
        in_specs=[pl.BlockSpec((D, tm), lambda i, j: (0, i)), hk, hk, hk, hk,
                  pl.BlockSpec((ec, D), lambda i, j: (j, 0)), pl.BlockSpec((D, ec), lambda i, j: (0, j))],
        out_specs=pl.BlockSpec((D, tm), lambda i, j: (0, i)), out_shape=jax.ShapeDtypeStruct((D, T), F32),
        scratch_shapes=[pltpu.VMEM((ec // 2, tm), F32), pltpu.VMEM((ec // 2, tm), F32)],
        compiler_params=_params(("parallel", "arbitrary")), name="peer_experts")(
            h2t, th1, e1, s2, e2, u, vt)


def _peer_finish_kernel(x_ref, yt_ref, o_ref):
    o_ref[...] = x_ref[...] + yt_ref[...].T


def _peer_finish(x, yt, tm):
    T, D = x.shape
    tok = pl.BlockSpec((tm, D), lambda i: (i, 0))
    return pl.pallas_call(
        _peer_finish_kernel, grid=(T // tm,), in_specs=[tok, pl.BlockSpec((D, tm), lambda i: (0, i))],
        out_specs=tok, out_shape=jax.ShapeDtypeStruct((T, D), F32),
        compiler_params=_params(("parallel",)), name="peer_finish")(x, yt)


def _row(v, width=None):
    v = v.astype(F32).reshape(1, -1)
    if width is not None and v.shape[1] < width:
        v = jnp.pad(v, ((0, 0), (0, width - v.shape[1])))
    return v


def _layer_weights(i, ln1, w_in, w_out, ln2, q_a_norm, w_qb, kv_a_norm, w_kb, w_vb, qn_nope, qn_rope, kn_nope,
                   kn_rope, m_conv_w, m_conv_b, m_dt_bias, m_A_log, m_D, m_norm, g_conv_w, g_dt_bias, g_A_log,
                   g_norm, p_wq, p_keys, p_u, p_v):
    D = w_in.shape[1]
    o = np.cumsum((0,) + IN_SIZES)
    wi = w_in[i]
    seg = lambda k: wi[:, o[k]:o[k + 1]]
    zeros = lambda n: jnp.zeros((D, n), F32)
    w_kvr = jnp.concatenate([seg(1), seg(2), zeros(LANES - A_ROPE)], axis=1)
    w_small = jnp.concatenate([seg(5), seg(8), seg(9), zeros(SMALL_W - M_HEADS - 2 * G_HEADS)], axis=1)
    in_ws = [w.astype(BF16) for w in (seg(0), w_kvr, seg(3), seg(4), seg(6), seg(7), w_small)]
    wqb = w_qb[i].reshape(Q_LORA, A_HEADS, A_NOPE + A_ROPE)
    w_qn = wqb[:, :, :A_NOPE].reshape(Q_LORA, A_HEADS * A_NOPE).astype(BF16)
    w_qr = wqb[:, :, A_NOPE:].reshape(Q_LORA, A_HEADS * A_ROPE).astype(BF16)
    wkb = w_kb[i]
    wvb = w_vb[i]
    mla_common = [_row(q_a_norm[i]), w_qn, w_qr, _row(qn_nope[i]), _row(jnp.tile(qn_rope[i], A_HEADS)),
                  _row(kv_a_norm[i]), _row(kn_rope[i], LANES), wkb.reshape(KV_LORA, A_HEADS * A_NOPE).astype(BF16),
                  _row(kn_nope[i])]
    mla_prompt = mla_common + [wvb.reshape(KV_LORA, A_HEADS * A_VDIM).astype(BF16)]
    mla_sample = mla_common + [jnp.transpose(wkb, (1, 2, 0)).astype(BF16)]
    wvb_heads = jnp.transpose(wvb, (1, 0, 2)).astype(BF16)
    col16 = lambda v, off: jnp.zeros((16, 1), F32).at[off:off + v.shape[0], 0].set(v.astype(F32))
    ssd = [m_conv_w[i].astype(F32), _row(m_conv_b[i]), _row(m_dt_bias[i], SMALL_W), _row(m_A_log[i], SMALL_W),
           m_dt_bias[i].astype(F32).reshape(M_HEADS, 1), m_A_log[i].astype(F32).reshape(M_HEADS, 1),
           _row(jnp.repeat(m_D[i], M_HEADDIM)), _row(m_norm[i])]
    pad_a = lambda v: jnp.zeros((1, SMALL_W), F32).at[0, G_A_COL:G_A_COL + G_HEADS].set(v.astype(F32))
    gdn = [g_conv_w[i].astype(F32), pad_a(g_dt_bias[i]), pad_a(g_A_log[i]),
           col16(g_dt_bias[i], G_A_COL), col16(g_A_log[i], G_A_COL), _row(g_norm[i])]
    wo = w_out[i].astype(BF16)
    aw = A_HEADS * A_VDIM
    out_ws = [wo[:aw], wo[aw:aw + M_WIDTH], wo[aw + M_WIDTH:]]
    peer = dict(ln2=_row(ln2[i]), wqt=p_wq[i].T.astype(BF16),
                keys=p_keys[i].reshape(2 * P_HEADS, P_NKEYS, P_DKEY // 2).astype(BF16),
                u=p_u[i].astype(BF16), vt=p_v[i].T.astype(BF16))
    return dict(ln1=_row(ln1[i]), in_ws=in_ws, mla_prompt=mla_prompt, mla_sample=mla_sample, wvb_heads=wvb_heads,
                ssd=ssd, gdn=gdn, out_ws=out_ws, peer=peer)


def _rope_tables(pos):
    half = A_ROPE // 2
    inv = ROPE_THETA ** (-jnp.arange(half, dtype=F32) / half)
    ang = pos.astype(F32)[:, None] * inv[None, :]
    cos, sin = jnp.cos(ang), jnp.sin(ang)
    reps = LANES // A_ROPE
    cos_t = jnp.tile(jnp.concatenate([cos, cos], axis=1), (1, reps))
    sin_t = jnp.tile(jnp.concatenate([-sin, sin], axis=1), (1, reps))
    return cos_t, sin_t


def _small_t(small, Bsz, L, c):
    s = small[:, :16].reshape(Bsz, L // c, c, 16)
    return jnp.swapaxes(s, 2, 3)


def _pick_tile(T, cap):
    t = cap
    while T % t:
        t //= 2
    return t


def _token_layer(x, Bsz, L, cos, sin, W, m_buf, m_s, g_buf, g_s, attend, sample):
    T = x.shape[0]
    tm = _pick_tile(T, 256)
    q_lat, kvr, m_z, m_xbc, g_qkv, g_z, small = _in_proj(x, W["ln1"], W["in_ws"], tm)
    mla = _mla_prep(q_lat, kvr, cos, sin, W["mla_sample"] if sample else W["mla_prompt"], tm, sample)
    a_out, cvec, krope, kinv = attend(mla)
    c = CHUNK if L % CHUNK == 0 else L
    smallt = _small_t(small, Bsz, L, c)
    r3 = lambda a: a.reshape(Bsz, L, a.shape[-1])
    m_y, m_buf, m_s = _ssd_scan(r3(m_xbc), r3(m_z), r3(small), smallt, m_buf, m_s, W["ssd"], c,
                                 _pick_tile(Bsz, 4 if sample else 2))
    g_y, g_buf, g_s = _gdn_scan(r3(g_qkv), r3(g_z), r3(small), smallt, g_buf, g_s, W["gdn"], c,
                                 _pick_tile(Bsz, 4))
    m_y = m_y.reshape(T, M_WIDTH)
    g_y = g_y.reshape(T, G_WIDTH)
    D = x.shape[1]
    if sample:
        (x,) = _rowwise(_out_proj_sample_body, [a_out, m_y, g_y, x], [W["wvb_heads"]] + W["out_ws"],
                        [(D, F32)], tm, "out_proj_sample")
    else:
        (x,) = _rowwise(_out_proj_prompt_body, [a_out, m_y, g_y, x], W["out_ws"], [(D, F32)], tm, "out_proj_prompt")
    pw = W["peer"]
    h2t, th1, e1, s2, e2 = _peer_query(x, pw["ln2"], pw["wqt"], pw["keys"], tm)
    tp = _pick_tile(T, 512)
    yt = _peer_experts(h2t, th1, e1, s2, e2, pw["u"], pw["vt"], tp, 1024)
    x = _peer_finish(x, yt, tp)
    return x, (cvec, krope, kinv, m_s, m_buf, g_s, g_buf)


def kernel(x_prompt, x_sample, cache_ckv, cache_krope, cache_kscale, state_ssm, state_ssm_conv, state_gdn, state_gdn_conv, page_table, ln1, w_in, w_out, ln2, q_a_norm, w_qb, kv_a_norm, w_kb, w_vb, qn_nope, qn_rope, kn_nope, kn_rope, m_conv_w, m_conv_b, m_dt_bias, m_A_log, m_D, m_norm, g_conv_w, g_dt_bias, g_A_log, g_norm, p_wq, p_keys, p_u, p_v):
    B, S, D = x_prompt.shape
    Bd, Ld, _ = x_sample.shape
    depth = ln1.shape[0]
    n_pages = page_table.shape[1]
    past = n_pages * PAGE_SIZE
    cos_p, sin_p = _rope_tables(jnp.arange(S))
    cos_s, sin_s = _rope_tables(past + jnp.arange(Ld))
    cos_s, sin_s = jnp.tile(cos_s, (Bd, 1)), jnp.tile(sin_s, (Bd, 1))
    krope_t = jnp.swapaxes(cache_krope, -1, -2)
    kscale_t = jnp.swapaxes(cache_kscale, -1, -2)
    npg = _pick_tile(n_pages, 32)
    tq = _pick_tile(S, 512)
    xp = x_prompt.reshape(B * S, D)
    xs = x_sample.reshape(Bd * Ld, D)
    new_p = [[] for _ in range(7)]
    new_s = [[] for _ in range(7)]
    weights = (ln1, w_in, w_out, ln2, q_a_norm, w_qb, kv_a_norm, w_kb, w_vb, qn_nope, qn_rope, kn_nope, kn_rope,
               m_conv_w, m_conv_b, m_dt_bias, m_A_log, m_D, m_norm, g_conv_w, g_dt_bias, g_A_log, g_norm,
               p_wq, p_keys, p_u, p_v)
    for i in range(depth):
        W = _layer_weights(i, *weights)

        def attend_prompt(mla):
            qcat, kcat, v, cvec, krope, kinv = mla
            return _flash_prompt(qcat, kcat, v, B, S, tq), cvec, krope, kinv

        def attend_sample(mla, i=i):
            qa, qr, cvec, krope, kinv = mla
            rows = Ld * A_HEADS
            padk = lambda a: jnp.pad(a.reshape(Bd, Ld, a.shape[-1]), ((0, 0), (0, NEW_PAD - Ld), (0, 0)))
            ksn = jnp.swapaxes(kinv.reshape(Bd, Ld, A_HEADS), 1, 2)
            ksn = jnp.pad(jnp.tile(ksn, (1, Ld, 1)), ((0, 0), (0, 0), (0, NEW_PAD - Ld)))
            o_lat = _decode_attend(page_table, qa.reshape(Bd, rows, KV_LORA), qr.reshape(Bd, rows, A_ROPE),
                                   padk(cvec), padk(krope), ksn, cache_ckv, krope_t, kscale_t, i, npg)
            return o_lat.reshape(Bd * Ld, A_HEADS * KV_LORA), cvec, krope, kinv

        zeros = lambda *s: jnp.zeros(s, F32)
        xp, st_p = _token_layer(xp, B, S, cos_p, sin_p, W, zeros(B, M_CONV - 1, M_CONV_CH),
                                zeros(B, M_HEADS, M_HEADDIM, M_STATE), zeros(B, G_CONV - 1, G_CONV_CH),
                                zeros(B, G_HEADS, G_DK, G_DV), attend_prompt, False)
        xs, st_s = _token_layer(xs, Bd, Ld, cos_s, sin_s, W, state_ssm_conv[i], state_ssm[i],
                                state_gdn_conv[i], state_gdn[i], attend_sample, True)
        for lst, val in zip(new_p, st_p):
            lst.append(val)
        for lst, val in zip(new_s, st_s):
            lst.append(val)

    def pack(vals, Bsz, L):
        cvec, krope, kinv, m_s, m_buf, g_s, g_buf = [jnp.stack(v) for v in vals]
        r = lambda a: a.reshape(depth, Bsz, L, a.shape[-1])
        return r(cvec), r(krope), r(kinv), m_s, m_buf, g_s, g_buf

    return (xp.reshape(B, S, D), xs.reshape(Bd, Ld, D)) + pack(new_p, B, S) + pack(new_s, Bd, Ld)
```

```python
import functools
import math

import jax
import jax.numpy as jnp
import numpy as np
from jax import lax
from jax.experimental import pallas as pl
from jax.experimental.pallas import tpu as pltpu

F32 = jnp.float32
BF16 = jnp.bfloat16
EPS = 1e-6

A_HEADS, A_NOPE, A_ROPE, A_VDIM = 8, 128, 64, 128
Q_LORA, KV_LORA = 384, 256
ROPE_THETA = 10000.0
M_HEADS, M_HEADDIM, M_WIDTH, M_GROUPS, M_STATE, M_CONV = 8, 64, 512, 2, 128, 4
M_CONV_CH = M_WIDTH + 2 * M_GROUPS * M_STATE
G_HEADS, G_DK, G_DV, G_WIDTH, G_CONV = 4, 128, 128, 512, 4
G_CONV_CH = 2 * G_HEADS * G_DK + G_WIDTH
IN_SIZES = (Q_LORA, KV_LORA, A_ROPE, M_WIDTH, M_CONV_CH, M_HEADS, G_CONV_CH, G_WIDTH, G_HEADS, G_HEADS)
P_HEADS, P_NKEYS, P_DKEY, P_TOPK = 8, 128, 256, 16
PAGE_SIZE = 128
CHUNK = 64
LANES = 128
SMALL_W = LANES
G_B_COL, G_A_COL = M_HEADS, M_HEADS + G_HEADS
VMEM_LIMIT = 56 * 1024 * 1024
ATT_SCALE = (A_NOPE + A_ROPE) ** -0.5


def _mm(a, b):
    return jnp.dot(a.astype(BF16), b.astype(BF16), preferred_element_type=F32)


def _mm_nt(a, b):
    return lax.dot_general(a.astype(BF16), b.astype(BF16), (((1,), (1,)), ((), ())),
                           preferred_element_type=F32)


def _mm_tn(a, b):
    return lax.dot_general(a.astype(BF16), b.astype(BF16), (((0,), (0,)), ((), ())),
                           preferred_element_type=F32)


def _hdot(a, b):
    return jnp.dot(a, b, precision=lax.Precision.HIGHEST, preferred_element_type=F32)


def _split3(x):
    hi = x.astype(BF16)
    r = x - hi.astype(F32)
    mid = r.astype(BF16)
    lo = (r - mid.astype(F32)).astype(BF16)
    return hi, mid, lo


def _dot_exact_left(sel, x):
    sb = sel.astype(BF16)
    return sum(jnp.dot(sb, p, preferred_element_type=F32) for p in _split3(x))


def _dot_exact_right(x, sel):
    sb = sel.astype(BF16)
    return sum(jnp.dot(p, sb, preferred_element_type=F32) for p in _split3(x))


def _dot3(a, b):
    ah = a.astype(BF16)
    al = (a - ah.astype(F32)).astype(BF16)
    bh = b.astype(BF16)
    bl = (b - bh.astype(F32)).astype(BF16)
    return (jnp.dot(ah, bh, preferred_element_type=F32) + jnp.dot(ah, bl, preferred_element_type=F32)
            + jnp.dot(al, bh, preferred_element_type=F32))


def _rms(x, g):
    return x * lax.rsqrt(jnp.mean(x * x, axis=-1, keepdims=True) + EPS) * g


def _softplus(x):
    return jnp.maximum(x, 0.0) + jnp.log1p(jnp.exp(-jnp.abs(x)))


def _silu(x):
    return x * jax.nn.sigmoid(x)


def _const_spec(a):
    nd = a.ndim
    return pl.BlockSpec(a.shape, lambda *_: (0,) * nd, pipeline_mode=pl.Buffered(1))


def _params(sem):
    return pltpu.CompilerParams(dimension_semantics=sem, vmem_limit_bytes=VMEM_LIMIT)


def _rowwise(body, rows, consts, outs, tm, name):
    T = rows[0].shape[0]
    assert T % tm == 0
    nr, nc = len(rows), len(consts)

    def kern(*refs):
        body(refs[:nr], refs[nr:nr + nc], refs[nr + nc:])

    in_specs = []
    for a in rows:
        assert a.shape[0] % tm == 0
        per = a.shape[0] // tm
        if a.shape[0] == T:
            in_specs.append(pl.BlockSpec((tm, a.shape[1]), lambda i: (i, 0)))
        else:
            in_specs.append(pl.BlockSpec((tm, a.shape[1]), lambda i, per=per: (i % per, 0)))
    in_specs += [_const_spec(a) for a in consts]
    out_specs = [pl.BlockSpec((tm, c), lambda i: (i, 0)) for c, _ in outs]
    out_shape = [jax.ShapeDtypeStruct((T, c), dt) for c, dt in outs]
    return pl.pallas_call(kern, grid=(T // tm,), in_specs=in_specs, out_specs=out_specs,
                          out_shape=out_shape, compiler_params=_params(("parallel",)),
                          name=name)(*rows, *consts)


def _in_proj_body(rows, consts, outs):
    x = rows[0][...]
    h = _rms(x, consts[0][...]).astype(BF16)
    for w_ref, o_ref in zip(consts[1:], outs):
        o_ref[...] = jnp.dot(h, w_ref[...], preferred_element_type=F32)


def _in_proj(x, ln1, ws, tm):
    outs = [(w.shape[1], F32) for w in ws]
    return _rowwise(_in_proj_body, [x], [ln1] + list(ws), outs, tm, "in_proj")


def _rope_rot(x, cos, sin_signed):
    w = x.shape[-1]
    lane = lax.broadcasted_iota(jnp.int32, x.shape, 1)
    first = (lane & (A_ROPE - 1)) < (A_ROPE // 2)
    rolled = jnp.where(first, pltpu.roll(x, w - A_ROPE // 2, 1), pltpu.roll(x, A_ROPE // 2, 1))
    return x * cos + rolled * sin_signed


def _mla_common(q_lat, kvr, cos, sin, qan, w_qn, w_qr, qn_rope, kvan, kn_rope, w_kb):
    ql = _rms(q_lat, qan).astype(BF16)
    qn = jnp.dot(ql, w_qn, preferred_element_type=F32)
    qr = jnp.dot(ql, w_qr, preferred_element_type=F32)
    wq = qr.shape[1]
    r = lax.broadcasted_iota(jnp.int32, (wq, wq), 0) >> 6
    c = lax.broadcasted_iota(jnp.int32, (wq, wq), 1) >> 6
    head_ones = (r == c).astype(F32)
    ss = _hdot(qr * qr, head_ones)
    qr = qr * lax.rsqrt(ss * (1.0 / A_ROPE) + EPS) * qn_rope
    reps = wq // LANES
    qr = _rope_rot(qr, jnp.tile(cos, (1, reps)), jnp.tile(sin, (1, reps)))
    kv = kvr[:, :KV_LORA]
    kr = kvr[:, KV_LORA:]
    cvec = _rms(kv, kvan)
    krn = kr * lax.rsqrt(jnp.sum(kr * kr, axis=-1, keepdims=True) * (1.0 / A_ROPE) + EPS) * kn_rope
    krope = _rope_rot(krn, cos, sin)
    kraw = jnp.dot(cvec.astype(BF16), w_kb, preferred_element_type=F32)
    return qn, qr, cvec, krope, kraw


def _k_inv_heads(kraw):
    tm = kraw.shape[0]
    lane8 = lax.broadcasted_iota(jnp.int32, (tm, A_HEADS), 1)
    kinv8 = jnp.zeros((tm, A_HEADS), F32)
    invs = []
    for h in range(A_HEADS):
        blk = kraw[:, h * A_NOPE:(h + 1) * A_NOPE]
        inv = lax.rsqrt(jnp.mean(blk * blk, axis=-1, keepdims=True) + EPS)
        invs.append(inv)
        kinv8 = jnp.where(lane8 == h, inv, kinv8)
    return invs, kinv8


def _mla_prompt_kernel(qlat_ref, kvr_ref, cos_ref, sin_ref, qan_ref, wqn_ref, wqr_ref, qnn_ref, qnr_ref,
                       kvan_ref, knr_ref, wkb_ref, knn_ref, wvb_ref,
                       qcat_ref, kcat_ref, v_ref, c_ref, krope_ref, kinv_ref):
    qn, qr, cvec, krope, kraw = _mla_common(
        qlat_ref[...], kvr_ref[...], cos_ref[...], sin_ref[...], qan_ref[...], wqn_ref[...], wqr_ref[...],
        qnr_ref[...], kvan_ref[...], knr_ref[...], wkb_ref[...])
    invs, kinv8 = _k_inv_heads(kraw)
    kr64 = krope[:, :A_ROPE]
    for h in range(A_HEADS):
        qn_h = _rms(qn[:, h * A_NOPE:(h + 1) * A_NOPE], qnn_ref[...])
        qcat_ref[h] = jnp.concatenate([qn_h, qr[:, h * A_ROPE:(h + 1) * A_ROPE]], axis=-1).astype(BF16)
        kn_h = kraw[:, h * A_NOPE:(h + 1) * A_NOPE] * invs[h] * knn_ref[...]
        kcat_ref[h] = jnp.concatenate([kn_h, kr64], axis=-1).astype(BF16)
    v_ref[...] = jnp.dot(cvec.astype(BF16), wvb_ref[...], preferred_element_type=F32).astype(BF16)
    c_ref[...] = cvec
    krope_ref[...] = kr64
    kinv_ref[...] = kinv8


def _mla_sample_kernel(qlat_ref, kvr_ref, cos_ref, sin_ref, qan_ref, wqn_ref, wqr_ref, qnn_ref, qnr_ref,
                       kvan_ref, knr_ref, wkb_ref, knn_ref, wkbt_ref,
                       qa_ref, qr_ref, c_ref, krope_ref, kinv_ref):
    qn, qr, cvec, krope, kraw = _mla_common(
        qlat_ref[...], kvr_ref[...], cos_ref[...], sin_ref[...], qan_ref[...], wqn_ref[...], wqr_ref[...],
        qnr_ref[...], kvan_ref[...], knr_ref[...], wkb_ref[...])
    _, kinv8 = _k_inv_heads(kraw)
    for h in range(A_HEADS):
        qn_h = _rms(qn[:, h * A_NOPE:(h + 1) * A_NOPE], qnn_ref[...]) * knn_ref[...]
        qa_ref[:, h * KV_LORA:(h + 1) * KV_LORA] = jnp.dot(
            qn_h.astype(BF16), wkbt_ref[h], preferred_element_type=F32).astype(BF16)
    qr_ref[...] = qr.astype(BF16)
    c_ref[...] = cvec
    krope_ref[...] = krope[:, :A_ROPE]
    kinv_ref[...] = kinv8


def _mla_prep(q_lat, kvr, cos, sin, consts, tm, sample):
    T = q_lat.shape[0]
    rows = [q_lat, kvr, cos, sin]
    in_specs = []
    for a in rows:
        per = a.shape[0] // tm
        if a.shape[0] == T:
            in_specs.append(pl.BlockSpec((tm, a.shape[1]), lambda i: (i, 0)))
        else:
            in_specs.append(pl.BlockSpec((tm, a.shape[1]), lambda i, per=per: (i % per, 0)))
    in_specs += [_const_spec(a) for a in consts]
    row_out = lambda c: pl.BlockSpec((tm, c), lambda i: (i, 0))
    tail_shapes = [jax.ShapeDtypeStruct((T, KV_LORA), F32), jax.ShapeDtypeStruct((T, A_ROPE), F32),
                   jax.ShapeDtypeStruct((T, A_HEADS), F32)]
    tail_specs = [row_out(KV_LORA), row_out(A_ROPE), row_out(A_HEADS)]
    if sample:
        kern = _mla_sample_kernel
        out_shape = [jax.ShapeDtypeStruct((T, A_HEADS * KV_LORA), BF16),
                     jax.ShapeDtypeStruct((T, A_HEADS * A_ROPE), BF16)] + tail_shapes
        out_specs = [row_out(A_HEADS * KV_LORA), row_out(A_HEADS * A_ROPE)] + tail_specs
    else:
        kern = _mla_prompt_kernel
        dqk = A_NOPE + A_ROPE
        head_spec = pl.BlockSpec((A_HEADS, tm, dqk), lambda i: (0, i, 0))
        out_shape = [jax.ShapeDtypeStruct((A_HEADS, T, dqk), BF16), jax.ShapeDtypeStruct((A_HEADS, T, dqk), BF16),
                     jax.ShapeDtypeStruct((T, A_HEADS * A_VDIM), BF16)] + tail_shapes
        out_specs = [head_spec, head_spec, row_out(A_HEADS * A_VDIM)] + tail_specs
    return pl.pallas_call(kern, grid=(T // tm,), in_specs=in_specs, out_specs=out_specs, out_shape=out_shape,
                          compiler_params=_params(("parallel",)),
                          name="mla_prep_sample" if sample else "mla_prep_prompt")(*rows, *consts)


def _flash_kernel(q_ref, k_ref, v_ref, o_ref, *, tq):
    i = pl.program_id(2)
    q = q_ref[0]
    row = lax.broadcasted_iota(jnp.int32, (tq, tq), 0)
    col = lax.broadcasted_iota(jnp.int32, (tq, tq), 1)

    def step(j, carry, diagonal):
        m, l, acc = carry
        start = pl.multiple_of(j * tq, tq)
        k = k_ref[0, pl.ds(start, tq), :]
        v = v_ref[pl.ds(start, tq), :]
        s = lax.dot_general(q, k, (((1,), (1,)), ((), ())), preferred_element_type=F32) * ATT_SCALE
        if diagonal:
            s = jnp.where(col <= row, s, -jnp.inf)
        m_new = jnp.maximum(m, jnp.max(s, axis=-1, keepdims=True))
        alpha = jnp.exp(m - m_new)
        p = jnp.exp(s - m_new)
        l = alpha * l + jnp.sum(p, axis=-1, keepdims=True)
        acc = alpha * acc + jnp.dot(p.astype(BF16), v, preferred_element_type=F32)
        return m_new, l, acc

    init = (jnp.full((tq, 1), -jnp.inf, F32), jnp.zeros((tq, 1), F32), jnp.zeros((tq, A_VDIM), F32))
    carry = lax.fori_loop(0, i, lambda j, c: step(j, c, False), init)
    _, l, acc = step(i, carry, True)
    o_ref[...] = (acc / l).astype(o_ref.dtype)


def _flash_prompt(qcat, kcat, v, Bsz, S, tq):
    nq = S // tq
    dqk = A_NOPE + A_ROPE
    T = Bsz * S
    return pl.pallas_call(
        functools.partial(_flash_kernel, tq=tq),
        grid=(Bsz, A_HEADS, nq),
        in_specs=[pl.BlockSpec((1, tq, dqk), lambda b, h, i: (h, b * nq + i, 0)),
                  pl.BlockSpec((1, S, dqk), lambda b, h, i: (h, b, 0)),
                  pl.BlockSpec((S, A_VDIM), lambda b, h, i: (b, h))],
        out_specs=pl.BlockSpec((tq, A_VDIM), lambda b, h, i: (b * nq + i, h)),
        out_shape=jax.ShapeDtypeStruct((T, A_HEADS * A_VDIM), BF16),
        compiler_params=_params(("parallel", "parallel", "arbitrary")),
        name="flash_prompt")(qcat, kcat, v)


NEW_PAD = 16


def _decode_kernel(pt_ref, qa_ref, qr_ref, cn_ref, krn_ref, ksn_ref, ckv_hbm, krt_hbm, kst_hbm, o_ref,
                   ck_buf, kr_buf, ks_buf, sem, *, li, npg):
    b = pl.program_id(0)
    nseq = pl.num_programs(0)
    rows = qa_ref.shape[1]
    reps = rows // A_HEADS
    ngroups = pt_ref.shape[1] // npg

    def group_copies(seq, g, slot):
        cps = []
        for p in range(npg):
            pid = pt_ref[seq, g * npg + p]
            cps.append(pltpu.make_async_copy(ckv_hbm.at[li, pid], ck_buf.at[slot, p], sem.at[slot, 0]))
            cps.append(pltpu.make_async_copy(krt_hbm.at[li, pid], kr_buf.at[slot, p], sem.at[slot, 1]))
            cps.append(pltpu.make_async_copy(kst_hbm.at[li, pid], ks_buf.at[slot, p], sem.at[slot, 2]))
        return cps

    @pl.when(b == 0)
    def _():
        for cp in group_copies(0, 0, 0):
            cp.start()

    qa = qa_ref[0]
    qr = qr_ref[0]
    nt = (((1,), (1,)), ((), ()))

    def body(g, carry):
        m_old, l_old, acc_old = carry
        gg = b * ngroups + g
        slot = gg & 1
        last = g + 1 == ngroups
        nxt_seq = jnp.where(last, b + 1, b)
        nxt_g = jnp.where(last, 0, g + 1)

        @pl.when(gg + 1 < nseq * ngroups)
        def _():
            for cp in group_copies(nxt_seq, nxt_g, 1 - slot):
                cp.start()

        for cp in group_copies(b, g, slot):
            cp.wait()
        m_run, l_run, acc = m_old, l_old, acc_old
        nsplit = 2 if npg % 2 == 0 else 1
        per = npg // nsplit
        for part in range(nsplit):
            ss, ccs = [], []
            for p in range(part * per, (part + 1) * per):
                cc = ck_buf[slot, p].astype(BF16)
                s = lax.dot_general(qa, cc, nt, preferred_element_type=F32)
                s = s * jnp.tile(ks_buf[slot, p], (reps, 1))
                s = s + jnp.dot(qr, kr_buf[slot, p].astype(BF16), preferred_element_type=F32)
                ss.append(s * ATT_SCALE)
                ccs.append(cc)
            smax = ss[0]
            for s in ss[1:]:
                smax = jnp.maximum(smax, s)
            m_new = jnp.maximum(m_run, jnp.max(smax, axis=-1, keepdims=True))
            alpha = jnp.exp(m_run - m_new)
            acc = alpha * acc
            esum = None
            for s, cc in zip(ss, ccs):
                e = jnp.exp(s - m_new)
                esum = e if esum is None else esum + e
                acc = acc + jnp.dot(e.astype(BF16), cc, preferred_element_type=F32)
            l_run = alpha * l_run + jnp.sum(esum, axis=-1, keepdims=True)
            m_run = m_new
        return m_run, l_run, acc

    init = (jnp.full((rows, 1), -jnp.inf, F32), jnp.zeros((rows, 1), F32), jnp.zeros((rows, KV_LORA), F32))
    m1, l1, acc1 = lax.fori_loop(0, ngroups, body, init)

    cn = cn_ref[0].astype(BF16)
    s = lax.dot_general(qa, cn, nt, preferred_element_type=F32) * ksn_ref[0]
    s = s + lax.dot_general(qr, krn_ref[0].astype(BF16), nt, preferred_element_type=F32)
    row = lax.broadcasted_iota(jnp.int32, s.shape, 0)
    col = lax.broadcasted_iota(jnp.int32, s.shape, 1)
    s = jnp.where(col <= (row >> 3), s * ATT_SCALE, -jnp.inf)
    m2 = jnp.maximum(m1, jnp.max(s, axis=-1, keepdims=True))
    a2 = jnp.exp(m1 - m2)
    e = jnp.exp(s - m2)
    l2 = a2 * l1 + jnp.sum(e, axis=-1, keepdims=True)
    acc2 = a2 * acc1 + jnp.dot(e.astype(BF16), cn, preferred_element_type=F32)
    o_ref[0] = acc2 / l2


def _decode_attend(page_table, qa, qr, cn, krn, ksn, cache_ckv, cache_krope_t, cache_kscale_t, li, npg):
    Bd, n_pages = page_table.shape
    rows = qa.shape[1]
    assert n_pages % npg == 0
    per_b = lambda b, pt: (b, 0, 0)
    hbm = pl.BlockSpec(memory_space=pl.ANY)
    in_specs = [pl.BlockSpec((1, rows, KV_LORA), per_b), pl.BlockSpec((1, rows, A_ROPE), per_b),
                pl.BlockSpec((1, NEW_PAD, KV_LORA), per_b), pl.BlockSpec((1, NEW_PAD, A_ROPE), per_b),
                pl.BlockSpec((1, rows, NEW_PAD), per_b), hbm, hbm, hbm]
    grid_spec = pltpu.PrefetchScalarGridSpec(
        num_scalar_prefetch=1, grid=(Bd,), in_specs=in_specs,
        out_specs=pl.BlockSpec((1, rows, KV_LORA), per_b),
        scratch_shapes=[pltpu.VMEM((2, npg, PAGE_SIZE, KV_LORA), F32), pltpu.VMEM((2, npg, A_ROPE, PAGE_SIZE), F32),
                        pltpu.VMEM((2, npg, A_HEADS, PAGE_SIZE), F32), pltpu.SemaphoreType.DMA((2, 3))])
    return pl.pallas_call(
        functools.partial(_decode_kernel, li=li, npg=npg), grid_spec=grid_spec,
        out_shape=jax.ShapeDtypeStruct((Bd, rows, KV_LORA), F32),
        compiler_params=_params(("arbitrary",)), name="decode_attend")(
            page_table, qa, qr, cn, krn, ksn, cache_ckv, cache_krope_t, cache_kscale_t)


def _tri_masks(c):
    r = lax.broadcasted_iota(jnp.int32, (c, c), 0)
    col = lax.broadcasted_iota(jnp.int32, (c, c), 1)
    return r, col


def _conv_silu(x_ref, buf0_ref, bufout_ref, xe_ref, w_ref, bias, j, c, b=0):
    @pl.when(j == 0)
    def _():
        xe_ref[b, 5:8, :] = buf0_ref[b]

    xe_ref[b, 8:8 + c, :] = x_ref[b]
    y = xe_ref[b, pl.ds(5, c), :] * w_ref[0:1, :]
    for k in range(1, 4):
        y = y + xe_ref[b, pl.ds(5 + k, c), :] * w_ref[k:k + 1, :]
    if bias is not None:
        y = y + bias
    bufout_ref[b] = xe_ref[b, c + 5:c + 8, :]
    tail = xe_ref[b, c:c + 8, :]
    xe_ref[b, 0:8, :] = tail
    return _silu(y)


def _ssd_kernel(xbc_ref, z_ref, small_ref, smallt_ref, buf0_ref, s0_ref, cw_ref, cb_ref, dtb_ref, alog_ref,
                dtbc_ref, alogc_ref, dskip_ref, ng_ref, y_ref, bufout_ref, sout_ref, xe_ref, *, c, nb):
    j = pl.program_id(1)

    @pl.when(j == 0)
    def _():
        sout_ref[...] = s0_ref[...]

    gw = M_GROUPS * M_STATE
    rep = M_HEADS // M_GROUPS
    r, col = _tri_masks(c)
    tri = col <= r
    xs, bm, cm, dt, dtt, cum, cumt, cb = [], [], [], [], [], [], [], []
    for b in range(nb):
        xbc = _conv_silu(xbc_ref, buf0_ref, bufout_ref, xe_ref, cw_ref, cb_ref[...], j, c, b)
        xs.append(xbc[:, :M_WIDTH])
        bm.append(xbc[:, M_WIDTH:M_WIDTH + gw])
        cm.append(xbc[:, M_WIDTH + gw:])
        dt.append(_softplus(small_ref[b] + dtb_ref[...]))
        dtt.append(_softplus(smallt_ref[b, 0][:M_HEADS] + dtbc_ref[...]))
        cum.append(_dot_exact_left(tri, dt[b] * (-jnp.exp(alog_ref[...]))))
        cumt.append(_dot_exact_right(dtt[b] * (-jnp.exp(alogc_ref[...])), r <= col))
        cb.append([_mm_nt(cm[b][:, g * M_STATE:(g + 1) * M_STATE], bm[b][:, g * M_STATE:(g + 1) * M_STATE])
                   for g in range(M_GROUPS)])
    chains = [(b, h) for b in range(nb) for h in range(M_HEADS)]
    grp = lambda t, b, h: t[b][:, (h // rep) * M_STATE:(h // rep + 1) * M_STATE]
    cc = [cum[b][:, h:h + 1] for b, h in chains]
    mmat = [cb[b][h // rep] * jnp.exp(jnp.where(tri, cc[i] - cumt[b][h:h + 1, :], -jnp.inf)) * dtt[b][h:h + 1, :]
            for i, (b, h) in enumerate(chains)]
    xh = [xs[b][:, h * M_HEADDIM:(h + 1) * M_HEADDIM] for b, h in chains]
    s_old = [sout_ref[b, h] for b, h in chains]
    y = [_mm(mmat[i], xh[i]) + _mm_nt(grp(cm, b, h), s_old[i]) * jnp.exp(cc[i])
         for i, (b, h) in enumerate(chains)]
    for i, (b, h) in enumerate(chains):
        clast = cum[b][c - 1:c, h:h + 1]
        to_end = jnp.exp(clast - cc[i]) * dt[b][:, h:h + 1]
        sout_ref[b, h] = s_old[i] * jnp.exp(clast) + _mm_tn(xh[i] * to_end, grp(bm, b, h))
    for b in range(nb):
        yb = jnp.concatenate(y[b * M_HEADS:(b + 1) * M_HEADS], axis=-1) + xs[b] * dskip_ref[...]
        y_ref[b] = _rms(yb * _silu(z_ref[b]), ng_ref[...])


def _ssd_scan(xbc, z, small, smallt, buf0, s0, consts, c, nb):
    Bsz, L, _ = xbc.shape
    nc = L // c
    assert Bsz % nb == 0
    blk = lambda w: pl.BlockSpec((nb, c, w), lambda b, j: (b, j, 0))
    buf_spec = pl.BlockSpec((nb, M_CONV - 1, M_CONV_CH), lambda b, j: (b, 0, 0))
    state_spec = pl.BlockSpec((nb, M_HEADS, M_HEADDIM, M_STATE), lambda b, j: (b, 0, 0, 0))
    in_specs = [blk(M_CONV_CH), blk(M_WIDTH), blk(SMALL_W),
                pl.BlockSpec((nb, 1, 16, c), lambda b, j: (b, j, 0, 0)), buf_spec, state_spec]
    in_specs += [_const_spec(a) for a in consts]
    out_shape = [jax.ShapeDtypeStruct((Bsz, L, M_WIDTH), F32),
                 jax.ShapeDtypeStruct((Bsz, M_CONV - 1, M_CONV_CH), F32),
                 jax.ShapeDtypeStruct((Bsz, M_HEADS, M_HEADDIM, M_STATE), F32)]
    return pl.pallas_call(
        functools.partial(_ssd_kernel, c=c, nb=nb), grid=(Bsz // nb, nc), in_specs=in_specs,
        out_specs=[blk(M_WIDTH), buf_spec, state_spec],
        out_shape=out_shape, scratch_shapes=[pltpu.VMEM((nb, c + 8, M_CONV_CH), F32)],
        compiler_params=_params(("parallel", "arbitrary")), name="ssd_scan")(
            xbc, z, small, smallt, buf0, s0, *consts)


def _unit_lower_inv_many(amats, c, r, col):
    eye = (r == col).astype(F32)
    same8 = (r >> 3) == (col >> 3)
    a8 = [jnp.where(same8, a, 0.0) for a in amats]
    x = [eye - a for a in a8]
    p = [_dot3(a, a) for a in a8]
    x = [xi + _dot3(xi, pi) for xi, pi in zip(x, p)]
    p = [_dot3(pi, pi) for pi in p]
    x = [xi + _dot3(xi, pi) for xi, pi in zip(x, p)]
    b = 8
    while b < c:
        sh = int(math.log2(b))
        pick = ((r >> (sh + 1)) == (col >> (sh + 1))) & ((r >> sh) != (col >> sh))
        low = [jnp.where(pick, a, 0.0) for a in amats]
        t = [_dot3(xi, li) for xi, li in zip(x, low)]
        x = [xi - _dot3(ti, xi) for xi, ti in zip(x, t)]
        b *= 2
    return x


def _gdn_kernel(qkv_ref, z_ref, small_ref, smallt_ref, buf0_ref, s0_ref, cw_ref, gdt_ref, galog_ref,
                gdtc_ref, galogc_ref, ng_ref, o_ref, bufout_ref, sout_ref, xe_ref, *, c, nb):
    j = pl.program_id(1)

    @pl.when(j == 0)
    def _():
        sout_ref[...] = s0_ref[...]

    nq = G_HEADS * G_DK
    r, col = _tri_masks(c)
    incl = col <= r
    strict = col < r
    chains = [(b, h) for b in range(nb) for h in range(G_HEADS)]
    qkv, small, gcum, gcumt = [], [], [], []
    for b in range(nb):
        qkv.append(_conv_silu(qkv_ref, buf0_ref, bufout_ref, xe_ref, cw_ref, None, j, c, b))
        small.append(small_ref[b])
        gall = -jnp.exp(galog_ref[...]) * _softplus(small[b] + gdt_ref[...])
        gallt = -jnp.exp(galogc_ref[...]) * _softplus(smallt_ref[b, 0] + gdtc_ref[...])
        gcum.append(_dot_exact_left(incl, gall))
        gcumt.append(_dot_exact_right(gallt, r <= col))
    q, k, v, beta, gc, decay = [], [], [], [], [], []
    for b, h in chains:
        qh = qkv[b][:, h * G_DK:(h + 1) * G_DK]
        kh = qkv[b][:, nq + h * G_DK:nq + (h + 1) * G_DK]
        q.append(qh * lax.rsqrt(jnp.sum(qh * qh, axis=-1, keepdims=True) + EPS) * (G_DK ** -0.5))
        k.append(kh * lax.rsqrt(jnp.sum(kh * kh, axis=-1, keepdims=True) + EPS))
        v.append(qkv[b][:, 2 * nq + h * G_DV:2 * nq + (h + 1) * G_DV])
        beta.append(jax.nn.sigmoid(small[b][:, G_B_COL + h:G_B_COL + h + 1]))
        gc.append(gcum[b][:, G_A_COL + h:G_A_COL + h + 1])
        gr = gcumt[b][G_A_COL + h:G_A_COL + h + 1, :]
        decay.append(jnp.exp(jnp.where(incl, gc[-1] - gr, -jnp.inf)))
    n = len(chains)
    kb = [k[i] * beta[i] for i in range(n)]
    amat = [jnp.where(strict, _mm_nt(kb[i], k[i]) * decay[i], 0.0) for i in range(n)]
    tmat = _unit_lower_inv_many(amat, c, r, col)
    u = [_mm(tmat[i], v[i] * beta[i]) for i in range(n)]
    w = [_mm(tmat[i], kb[i] * jnp.exp(gc[i])) for i in range(n)]
    attn = [jnp.where(incl, _mm_nt(q[i], k[i]) * decay[i], 0.0) for i in range(n)]
    s_old = [sout_ref[b, h] for b, h in chains]
    v_new = [u[i] - _mm(w[i], s_old[i]) for i in range(n)]
    o = [_mm(q[i] * jnp.exp(gc[i]), s_old[i]) + _mm(attn[i], v_new[i]) for i in range(n)]
    for i, (b, h) in enumerate(chains):
        glast = gcum[b][c - 1:c, G_A_COL + h:G_A_COL + h + 1]
        sout_ref[b, h] = s_old[i] * jnp.exp(glast) + _mm_tn(k[i] * jnp.exp(glast - gc[i]), v_new[i])
    for b in range(nb):
        z = z_ref[b]
        outs = [_rms(o[b * G_HEADS + h], ng_ref[...]) * _silu(z[:, h * G_DV:(h + 1) * G_DV])
                for h in range(G_HEADS)]
        o_ref[b] = jnp.concatenate(outs, axis=-1)


def _gdn_scan(qkv, z, small, smallt, buf0, s0, consts, c, nb):
    Bsz, L, _ = qkv.shape
    nc = L // c
    assert Bsz % nb == 0
    blk = lambda w: pl.BlockSpec((nb, c, w), lambda b, j: (b, j, 0))
    buf_spec = pl.BlockSpec((nb, G_CONV - 1, G_CONV_CH), lambda b, j: (b, 0, 0))
    state_spec = pl.BlockSpec((nb, G_HEADS, G_DK, G_DV), lambda b, j: (b, 0, 0, 0))
    in_specs = [blk(G_CONV_CH), blk(G_WIDTH), blk(SMALL_W),
                pl.BlockSpec((nb, 1, 16, c), lambda b, j: (b, j, 0, 0)), buf_spec, state_spec]
    in_specs += [_const_spec(a) for a in consts]
    out_shape = [jax.ShapeDtypeStruct((Bsz, L, G_WIDTH), F32),
                 jax.ShapeDtypeStruct((Bsz, G_CONV - 1, G_CONV_CH), F32),
                 jax.ShapeDtypeStruct((Bsz, G_HEADS, G_DK, G_DV), F32)]
    return pl.pallas_call(
        functools.partial(_gdn_kernel, c=c, nb=nb), grid=(Bsz // nb, nc), in_specs=in_specs,
        out_specs=[blk(G_WIDTH), buf_spec, state_spec],
        out_shape=out_shape, scratch_shapes=[pltpu.VMEM((nb, c + 8, G_CONV_CH), F32)],
        compiler_params=_params(("parallel", "arbitrary")), name="gdn_scan")(
            qkv, z, small, smallt, buf0, s0, *consts)


def _out_proj_prompt_body(rows, consts, outs):
    a, m, g, x = rows
    wa, wm, wg = consts
    outs[0][...] = x[...] + _mm(a[...], wa[...]) + _mm(m[...], wm[...]) + _mm(g[...], wg[...])


def _out_proj_sample_body(rows, consts, outs):
    olat, m, g, x = rows
    wvb, wa, wm, wg = consts
    heads = [_mm(olat[:, h * KV_LORA:(h + 1) * KV_LORA], wvb[h]) for h in range(A_HEADS)]
    a = jnp.concatenate(heads, axis=-1)
    outs[0][...] = x[...] + _mm(a, wa[...]) + _mm(m[...], wm[...]) + _mm(g[...], wg[...])


def _top_values(s, k):
    out = []
    for it in range(k):
        m = jnp.max(s, axis=0, keepdims=True)
        out.append(m)
        if it + 1 < k:
            s = jnp.where(s == m, -jnp.inf, s)
    return out


def _peer_query_kernel(x_ref, ln2_ref, wqt_ref, keys_ref, h2t_ref, th1_ref, e1_ref, s2_ref, e2_ref):
    h2t = _rms(x_ref[...], ln2_ref[...]).T.astype(BF16)
    h2t_ref[...] = h2t
    qt = jnp.dot(wqt_ref[...], h2t, preferred_element_type=F32).astype(BF16)
    half = P_DKEY // 2
    for h in range(P_HEADS):
        s1 = jnp.dot(keys_ref[2 * h], qt[(2 * h) * half:(2 * h + 1) * half], preferred_element_type=F32)
        s2 = jnp.dot(keys_ref[2 * h + 1], qt[(2 * h + 1) * half:(2 * h + 2) * half], preferred_element_type=F32)
        v1 = _top_values(s1, P_TOPK)
        v2 = _top_values(s2, P_TOPK)
        v1m = jnp.concatenate(v1, axis=0)
        v2m = jnp.concatenate(v2, axis=0)
        hk = P_TOPK // 2
        cand = jnp.concatenate([v1[0] + v2m] + [v1[a] + v2m[:hk] for a in range(1, hk)] + [v1m[hk:] + v2[0]],
                               axis=0)
        best = _top_values(cand, P_TOPK)
        zsum = jnp.ones_like(best[0])
        for b in best[1:]:
            zsum = zsum + jnp.exp(b - best[0])
        th1_ref[h] = best[P_TOPK - 1] - s1
        s2_ref[h] = s2
        e1_ref[h] = jnp.exp(s1 - v1[0])
        e2_ref[h] = (jnp.exp(s2 - v2[0]) * (0.5 / zsum)).astype(BF16)


def _peer_query(x, ln2, wqt, keys, tm):
    T, D = x.shape
    hk = pl.BlockSpec((P_HEADS, P_NKEYS, tm), lambda i: (0, 0, i))
    hk_shape = jax.ShapeDtypeStruct((P_HEADS, P_NKEYS, T), F32)
    return pl.pallas_call(
        _peer_query_kernel, grid=(T // tm,),
        in_specs=[pl.BlockSpec((tm, D), lambda i: (i, 0)), _const_spec(ln2), _const_spec(wqt), _const_spec(keys)],
        out_specs=[pl.BlockSpec((D, tm), lambda i: (0, i)), hk, hk, hk, hk],
        out_shape=[jax.ShapeDtypeStruct((D, T), BF16), hk_shape, hk_shape, hk_shape,
                   jax.ShapeDtypeStruct((P_HEADS, P_NKEYS, T), BF16)],
        compiler_params=_params(("parallel",)), name="peer_query")(x, ln2, wqt, keys)


def _gelu_x2(x):
    return x * (1.0 + lax.erf(x * (2.0 ** -0.5)))


def _peer_gate(th1_ref, e1_ref, s2_ref, e2_ref, i1):
    g = None
    for h in range(P_HEADS):
        keep = s2_ref[h] >= th1_ref[h, pl.ds(i1, 1), :]
        w = jnp.where(keep, e2_ref[h] * e1_ref[h, pl.ds(i1, 1), :].astype(BF16), jnp.zeros((), BF16))
        g = w if g is None else g + w
    return g


def _peer_expert_kernel(h2t_ref, th1_ref, e1_ref, s2_ref, e2_ref, u_ref, vt_ref, y_ref, ga_ref, gb_ref, *, ec):
    j = pl.program_id(1)
    nj = pl.num_programs(1)
    half = ec // 2
    nsub = half // P_NKEYS

    def fill(g_ref, i1_base):
        for r in range(nsub):
            g_ref[r * P_NKEYS:(r + 1) * P_NKEYS, :] = _peer_gate(th1_ref, e1_ref, s2_ref, e2_ref, i1_base + r)

    @pl.when(j == 0)
    def _():
        fill(ga_ref, 0)
        y_ref[...] = jnp.zeros(y_ref.shape, F32)

    pt_a = jnp.dot(u_ref[:half, :], h2t_ref[...], preferred_element_type=F32)
    fill(gb_ref, j * 2 * nsub + nsub)
    act_a = _gelu_x2(pt_a).astype(BF16) * ga_ref[...]
    pt_b = jnp.dot(u_ref[half:, :], h2t_ref[...], preferred_element_type=F32)
    fill(ga_ref, jnp.minimum(j + 1, nj - 1) * 2 * nsub)
    act_b = _gelu_x2(pt_b).astype(BF16) * gb_ref[...]
    act = jnp.concatenate([act_a, act_b], axis=0)
    y_ref[...] += jnp.dot(vt_ref[...], act, preferred_element_type=F32)


def _peer_experts(h2t, th1, e1, s2, e2, u, vt, tm, ec):
    D, T = h2t.shape
    ne = u.shape[0]
    hk = pl.BlockSpec((P_HEADS, P_NKEYS, tm), lambda i, j: (0, 0, i))
    return pl.pallas_call(
        functools.partial(_peer_expert_kernel, ec=ec), grid=(T // tm, ne // ec),
        in_specs=[pl.BlockSpec((D, tm), lambda i, j: (0, i)), hk, hk, hk, hk,
                  pl.BlockSpec((ec, D), lambda i, j: (j, 0)), pl.BlockSpec((D, ec), lambda i, j: (0, j))],
        out_specs=pl.BlockSpec((D, tm), lambda i, j: (0, i)), out_shape=jax.ShapeDtypeStruct((D, T), F32),
        scratch_shapes=[pltpu.VMEM((ec // 2, tm), BF16), pltpu.VMEM((ec // 2, tm), BF16)],
        compiler_params=_params(("parallel", "arbitrary")), name="peer_experts")(
            h2t, th1, e1, s2, e2, u, vt)


def _peer_finish_kernel(x_ref, yt_ref, o_ref):
    o_ref[...] = x_ref[...] + yt_ref[...].T


def _peer_finish(x, yt, tm):
    T, D = x.shape
    tok = pl.BlockSpec((tm, D), lambda i: (i, 0))
    return pl.pallas_call(
        _peer_finish_kernel, grid=(T // tm,), in_specs=[tok, pl.BlockSpec((D, tm), lambda i: (0, i))],
        out_specs=tok, out_shape=jax.ShapeDtypeStruct((T, D), F32),
        compiler_params=_params(("parallel",)), name="peer_finish")(x, yt)


def _row(v, width=None):
    v = v.astype(F32).reshape(1, -1)
    if width is not None and v.shape[1] < width:
        v = jnp.pad(v, ((0, 0), (0, width - v.shape[1])))
    return v


def _layer_weights(i, ln1, w_in, w_out, ln2, q_a_norm, w_qb, kv_a_norm, w_kb, w_vb, qn_nope, qn_rope, kn_nope,
                   kn_rope, m_conv_w, m_conv_b, m_dt_bias, m_A_log, m_D, m_norm, g_conv_w, g_dt_bias, g_A_log,
                   g_norm, p_wq, p_keys, p_u, p_v):
    D = w_in.shape[1]
    o = np.cumsum((0,) + IN_SIZES)
    wi = w_in[i]
    seg = lambda k: wi[:, o[k]:o[k + 1]]
    zeros = lambda n: jnp.zeros((D, n), F32)
    w_kvr = jnp.concatenate([seg(1), seg(2), zeros(LANES - A_ROPE)], axis=1)
    w_small = jnp.concatenate([seg(5), seg(8), seg(9), zeros(SMALL_W - M_HEADS - 2 * G_HEADS)], axis=1)
    in_ws = [w.astype(BF16) for w in (seg(0), w_kvr, seg(3), seg(4), seg(6), seg(7), w_small)]
    wqb = w_qb[i].reshape(Q_LORA, A_HEADS, A_NOPE + A_ROPE)
    w_qn = wqb[:, :, :A_NOPE].reshape(Q_LORA, A_HEADS * A_NOPE).astype(BF16)
    w_qr = wqb[:, :, A_NOPE:].reshape(Q_LORA, A_HEADS * A_ROPE).astype(BF16)
    wkb = w_kb[i]
    wvb = w_vb[i]
    mla_common = [_row(q_a_norm[i]), w_qn, w_qr, _row(qn_nope[i]), _row(jnp.tile(qn_rope[i], A_HEADS)),
                  _row(kv_a_norm[i]), _row(kn_rope[i], LANES), wkb.reshape(KV_LORA, A_HEADS * A_NOPE).astype(BF16),
                  _row(kn_nope[i])]
    mla_prompt = mla_common + [wvb.reshape(KV_LORA, A_HEADS * A_VDIM).astype(BF16)]
    mla_sample = mla_common + [jnp.transpose(wkb, (1, 2, 0)).astype(BF16)]
    wvb_heads = jnp.transpose(wvb, (1, 0, 2)).astype(BF16)
    col16 = lambda v, off: jnp.zeros((16, 1), F32).at[off:off + v.shape[0], 0].set(v.astype(F32))
    ssd = [m_conv_w[i].astype(F32), _row(m_conv_b[i]), _row(m_dt_bias[i], SMALL_W), _row(m_A_log[i], SMALL_W),
           m_dt_bias[i].astype(F32).reshape(M_HEADS, 1), m_A_log[i].astype(F32).reshape(M_HEADS, 1),
           _row(jnp.repeat(m_D[i], M_HEADDIM)), _row(m_norm[i])]
    pad_a = lambda v: jnp.zeros((1, SMALL_W), F32).at[0, G_A_COL:G_A_COL + G_HEADS].set(v.astype(F32))
    gdn = [g_conv_w[i].astype(F32), pad_a(g_dt_bias[i]), pad_a(g_A_log[i]),
           col16(g_dt_bias[i], G_A_COL), col16(g_A_log[i], G_A_COL), _row(g_norm[i])]
    wo = w_out[i].astype(BF16)
    aw = A_HEADS * A_VDIM
    out_ws = [wo[:aw], wo[aw:aw + M_WIDTH], wo[aw + M_WIDTH:]]
    peer = dict(ln2=_row(ln2[i]), wqt=p_wq[i].T.astype(BF16),
                keys=p_keys[i].reshape(2 * P_HEADS, P_NKEYS, P_DKEY // 2).astype(BF16),
                u=p_u[i].astype(BF16), vt=p_v[i].T.astype(BF16))
    return dict(ln1=_row(ln1[i]), in_ws=in_ws, mla_prompt=mla_prompt, mla_sample=mla_sample, wvb_heads=wvb_heads,
                ssd=ssd, gdn=gdn, out_ws=out_ws, peer=peer)


def _rope_tables(pos):
    half = A_ROPE // 2
    inv = ROPE_THETA ** (-jnp.arange(half, dtype=F32) / half)
    ang = pos.astype(F32)[:, None] * inv[None, :]
    cos, sin = jnp.cos(ang), jnp.sin(ang)
    reps = LANES // A_ROPE
    cos_t = jnp.tile(jnp.concatenate([cos, cos], axis=1), (1, reps))
    sin_t = jnp.tile(jnp.concatenate([-sin, sin], axis=1), (1, reps))
    return cos_t, sin_t


def _small_t(small, Bsz, L, c):
    s = small[:, :16].reshape(Bsz, L // c, c, 16)
    return jnp.swapaxes(s, 2, 3)


def _pick_tile(T, cap):
    t = cap
    while T % t:
        t //= 2
    return t


def _token_layer(x, Bsz, L, cos, sin, W, m_buf, m_s, g_buf, g_s, attend, sample):
    T = x.shape[0]
    tm = _pick_tile(T, 256)
    q_lat, kvr, m_z, m_xbc, g_qkv, g_z, small = _in_proj(x, W["ln1"], W["in_ws"], tm)
    mla = _mla_prep(q_lat, kvr, cos, sin, W["mla_sample"] if sample else W["mla_prompt"], tm, sample)
    a_out, cvec, krope, kinv = attend(mla)
    c = CHUNK if L % CHUNK == 0 else L
    smallt = _small_t(small, Bsz, L, c)
    r3 = lambda a: a.reshape(Bsz, L, a.shape[-1])
    m_y, m_buf, m_s = _ssd_scan(r3(m_xbc), r3(m_z), r3(small), smallt, m_buf, m_s, W["ssd"], c,
                                 _pick_tile(Bsz, 4 if sample else 2))
    g_y, g_buf, g_s = _gdn_scan(r3(g_qkv), r3(g_z), r3(small), smallt, g_buf, g_s, W["gdn"], c,
                                 _pick_tile(Bsz, 4))
    m_y = m_y.reshape(T, M_WIDTH)
    g_y = g_y.reshape(T, G_WIDTH)
    D = x.shape[1]
    if sample:
        (x,) = _rowwise(_out_proj_sample_body, [a_out, m_y, g_y, x], [W["wvb_heads"]] + W["out_ws"],
                        [(D, F32)], tm, "out_proj_sample")
    else:
        (x,) = _rowwise(_out_proj_prompt_body, [a_out, m_y, g_y, x], W["out_ws"], [(D, F32)], tm, "out_proj_prompt")
    pw = W["peer"]
    h2t, th1, e1, s2, e2 = _peer_query(x, pw["ln2"], pw["wqt"], pw["keys"], tm)
    tp = _pick_tile(T, 512)
    yt = _peer_experts(h2t, th1, e1, s2, e2, pw["u"], pw["vt"], tp, 1024)
    x = _peer_finish(x, yt, tp)
    return x, (cvec, krope, kinv, m_s, m_buf, g_s, g_buf)


def kernel(x_prompt, x_sample, cache_ckv, cache_krope, cache_kscale, state_ssm, state_ssm_conv, state_gdn, state_gdn_conv, page_table, ln1, w_in, w_out, ln2, q_a_norm, w_qb, kv_a_norm, w_kb, w_vb, qn_nope, qn_rope, kn_nope, kn_rope, m_conv_w, m_conv_b, m_dt_bias, m_A_log, m_D, m_norm, g_conv_w, g_dt_bias, g_A_log, g_norm, p_wq, p_keys, p_u, p_v):
    B, S, D = x_prompt.shape
    Bd, Ld, _ = x_sample.shape
    depth = ln1.shape[0]
    n_pages = page_table.shape[1]
    past = n_pages * PAGE_SIZE
    cos_p, sin_p = _rope_tables(jnp.arange(S))
    cos_s, sin_s = _rope_tables(past + jnp.arange(Ld))
    cos_s, sin_s = jnp.tile(cos_s, (Bd, 1)), jnp.tile(sin_s, (Bd, 1))
    krope_t = jnp.swapaxes(cache_krope, -1, -2)
    kscale_t = jnp.swapaxes(cache_kscale, -1, -2)
    npg = _pick_tile(n_pages, 32)
    tq = _pick_tile(S, 512)
    xp = x_prompt.reshape(B * S, D)
    xs = x_sample.reshape(Bd * Ld, D)
    new_p = [[] for _ in range(7)]
    new_s = [[] for _ in range(7)]
    weights = (ln1, w_in, w_out, ln2, q_a_norm, w_qb, kv_a_norm, w_kb, w_vb, qn_nope, qn_rope, kn_nope, kn_rope,
               m_conv_w, m_conv_b, m_dt_bias, m_A_log, m_D, m_norm, g_conv_w, g_dt_bias, g_A_log, g_norm,
               p_wq, p_keys, p_u, p_v)
    for i in range(depth):
        W = _layer_weights(i, *weights)

        def attend_prompt(mla):
            qcat, kcat, v, cvec, krope, kinv = mla
            return _flash_prompt(qcat, kcat, v, B, S, tq), cvec, krope, kinv

        def attend_sample(mla, i=i):
            qa, qr, cvec, krope, kinv = mla
            rows = Ld * A_HEADS
            padk = lambda a: jnp.pad(a.reshape(Bd, Ld, a.shape[-1]), ((0, 0), (0, NEW_PAD - Ld), (0, 0)))
            ksn = jnp.swapaxes(kinv.reshape(Bd, Ld, A_HEADS), 1, 2)
            ksn = jnp.pad(jnp.tile(ksn, (1, Ld, 1)), ((0, 0), (0, 0), (0, NEW_PAD - Ld)))
            o_lat = _decode_attend(page_table, qa.reshape(Bd, rows, KV_LORA), qr.reshape(Bd, rows, A_ROPE),
                                   padk(cvec), padk(krope), ksn, cache_ckv, krope_t, kscale_t, i, npg)
            return o_lat.reshape(Bd * Ld, A_HEADS * KV_LORA), cvec, krope, kinv

        zeros = lambda *s: jnp.zeros(s, F32)
        xp, st_p = _token_layer(xp, B, S, cos_p, sin_p, W, zeros(B, M_CONV - 1, M_CONV_CH),
                                zeros(B, M_HEADS, M_HEADDIM, M_STATE), zeros(B, G_CONV - 1, G_CONV_CH),
                                zeros(B, G_HEADS, G_DK, G_DV), attend_prompt, False)
        xs, st_s = _token_layer(xs, Bd, Ld, cos_s, sin_s, W, state_ssm_conv[i], state_ssm[i],
                                state_gdn_conv[i], state_gdn[i], attend_sample, True)
        for lst, val in zip(new_p, st_p):
            lst.append(val)
        for lst, val in zip(new_s, st_s):
            lst.append(val)

    def pack(vals, Bsz, L):
        cvec, krope, kinv, m_s, m_buf, g_s, g_buf = [jnp.stack(v) for v in vals]
        r = lambda a: a.reshape(depth, Bsz, L, a.shape[-1])
        return r(cvec), r(krope), r(kinv), m_s, m_buf, g_s, g_buf

    return (xp.reshape(B, S, D), xs.reshape(Bd, Ld, D)) + pack(new_p, B, S) + pack(new_s, Bd, Ld)
```

```python
import functools
import math

import jax
import jax.numpy as jnp
import numpy as np
from jax import lax
from jax.experimental import pallas as pl
from jax.experimental.pallas import tpu as pltpu

F32 = jnp.float32
BF16 = jnp.bfloat16
EPS = 1e-6

A_HEADS, A_NOPE, A_ROPE, A_VDIM = 8, 128, 64, 128
Q_LORA, KV_LORA = 384, 256
ROPE_THETA = 10000.0
M_HEADS, M_HEADDIM, M_WIDTH, M_GROUPS, M_STATE, M_CONV = 8, 64, 512, 2, 128, 4
M_CONV_CH = M_WIDTH + 2 * M_GROUPS * M_STATE
G_HEADS, G_DK, G_DV, G_WIDTH, G_CONV = 4, 128, 128, 512, 4
G_CONV_CH = 2 * G_HEADS * G_DK + G_WIDTH
IN_SIZES = (Q_LORA, KV_LORA, A_ROPE, M_WIDTH, M_CONV_CH, M_HEADS, G_CONV_CH, G_WIDTH, G_HEADS, G_HEADS)
P_HEADS, P_NKEYS, P_DKEY, P_TOPK = 8, 128, 256, 16
PAGE_SIZE = 128
CHUNK = 64
LANES = 128
SMALL_W = LANES
G_B_COL, G_A_COL = M_HEADS, M_HEADS + G_HEADS
VMEM_LIMIT = 56 * 1024 * 1024
ATT_SCALE = (A_NOPE + A_ROPE) ** -0.5


def _mm(a, b):
    return jnp.dot(a.astype(BF16), b.astype(BF16), preferred_element_type=F32)


def _mm_nt(a, b):
    return lax.dot_general(a.astype(BF16), b.astype(BF16), (((1,), (1,)), ((), ())),
                           preferred_element_type=F32)


def _mm_tn(a, b):
    return lax.dot_general(a.astype(BF16), b.astype(BF16), (((0,), (0,)), ((), ())),
                           preferred_element_type=F32)


def _hdot(a, b):
    return jnp.dot(a, b, precision=lax.Precision.HIGHEST, preferred_element_type=F32)


def _split3(x):
    hi = x.astype(BF16)
    r = x - hi.astype(F32)
    mid = r.astype(BF16)
    lo = (r - mid.astype(F32)).astype(BF16)
    return hi, mid, lo


def _dot_exact_left(sel, x):
    sb = sel.astype(BF16)
    return sum(jnp.dot(sb, p, preferred_element_type=F32) for p in _split3(x))


def _dot_exact_right(x, sel):
    sb = sel.astype(BF16)
    return sum(jnp.dot(p, sb, preferred_element_type=F32) for p in _split3(x))


def _dot3(a, b):
    ah = a.astype(BF16)
    al = (a - ah.astype(F32)).astype(BF16)
    bh = b.astype(BF16)
    bl = (b - bh.astype(F32)).astype(BF16)
    return (jnp.dot(ah, bh, preferred_element_type=F32) + jnp.dot(ah, bl, preferred_element_type=F32)
            + jnp.dot(al, bh, preferred_element_type=F32))


def _rms(x, g):
    return x * lax.rsqrt(jnp.mean(x * x, axis=-1, keepdims=True) + EPS) * g


def _softplus(x):
    return jnp.maximum(x, 0.0) + jnp.log1p(jnp.exp(-jnp.abs(x)))


def _silu(x):
    return x * jax.nn.sigmoid(x)


def _const_spec(a):
    nd = a.ndim
    return pl.BlockSpec(a.shape, lambda *_: (0,) * nd, pipeline_mode=pl.Buffered(1))


def _params(sem):
    return pltpu.CompilerParams(dimension_semantics=sem, vmem_limit_bytes=VMEM_LIMIT)


def _rowwise(body, rows, consts, outs, tm, name):
    T = rows[0].shape[0]
    assert T % tm == 0
    nr, nc = len(rows), len(consts)

    def kern(*refs):
        body(refs[:nr], refs[nr:nr + nc], refs[nr + nc:])

    in_specs = []
    for a in rows:
        assert a.shape[0] % tm == 0
        per = a.shape[0] // tm
        if a.shape[0] == T:
            in_specs.append(pl.BlockSpec((tm, a.shape[1]), lambda i: (i, 0)))
        else:
            in_specs.append(pl.BlockSpec((tm, a.shape[1]), lambda i, per=per: (i % per, 0)))
    in_specs += [_const_spec(a) for a in consts]
    out_specs = [pl.BlockSpec((tm, c), lambda i: (i, 0)) for c, _ in outs]
    out_shape = [jax.ShapeDtypeStruct((T, c), dt) for c, dt in outs]
    return pl.pallas_call(kern, grid=(T // tm,), in_specs=in_specs, out_specs=out_specs,
                          out_shape=out_shape, compiler_params=_params(("parallel",)),
                          name=name)(*rows, *consts)


def _in_proj_body(rows, consts, outs):
    x = rows[0][...]
    h = _rms(x, consts[0][...]).astype(BF16)
    for w_ref, o_ref in zip(consts[1:], outs):
        o_ref[...] = jnp.dot(h, w_ref[...], preferred_element_type=F32)


def _in_proj(x, ln1, ws, tm):
    outs = [(w.shape[1], F32) for w in ws]
    return _rowwise(_in_proj_body, [x], [ln1] + list(ws), outs, tm, "in_proj")


def _rope_rot(x, cos, sin_signed):
    w = x.shape[-1]
    lane = lax.broadcasted_iota(jnp.int32, x.shape, 1)
    first = (lane & (A_ROPE - 1)) < (A_ROPE // 2)
    rolled = jnp.where(first, pltpu.roll(x, w - A_ROPE // 2, 1), pltpu.roll(x, A_ROPE // 2, 1))
    return x * cos + rolled * sin_signed


def _mla_common(q_lat, kvr, cos, sin, qan, w_qn, w_qr, qn_rope, kvan, kn_rope, w_kb):
    ql = _rms(q_lat, qan).astype(BF16)
    qn = jnp.dot(ql, w_qn, preferred_element_type=F32)
    qr = jnp.dot(ql, w_qr, preferred_element_type=F32)
    wq = qr.shape[1]
    r = lax.broadcasted_iota(jnp.int32, (wq, wq), 0) >> 6
    c = lax.broadcasted_iota(jnp.int32, (wq, wq), 1) >> 6
    head_ones = (r == c).astype(F32)
    ss = _hdot(qr * qr, head_ones)
    qr = qr * lax.rsqrt(ss * (1.0 / A_ROPE) + EPS) * qn_rope
    reps = wq // LANES
    qr = _rope_rot(qr, jnp.tile(cos, (1, reps)), jnp.tile(sin, (1, reps)))
    kv = kvr[:, :KV_LORA]
    kr = kvr[:, KV_LORA:]
    cvec = _rms(kv, kvan)
    krn = kr * lax.rsqrt(jnp.sum(kr * kr, axis=-1, keepdims=True) * (1.0 / A_ROPE) + EPS) * kn_rope
    krope = _rope_rot(krn, cos, sin)
    kraw = jnp.dot(cvec.astype(BF16), w_kb, preferred_element_type=F32)
    return qn, qr, cvec, krope, kraw


def _k_inv_heads(kraw):
    tm = kraw.shape[0]
    lane8 = lax.broadcasted_iota(jnp.int32, (tm, A_HEADS), 1)
    kinv8 = jnp.zeros((tm, A_HEADS), F32)
    invs = []
    for h in range(A_HEADS):
        blk = kraw[:, h * A_NOPE:(h + 1) * A_NOPE]
        inv = lax.rsqrt(jnp.mean(blk * blk, axis=-1, keepdims=True) + EPS)
        invs.append(inv)
        kinv8 = jnp.where(lane8 == h, inv, kinv8)
    return invs, kinv8


def _mla_prompt_kernel(qlat_ref, kvr_ref, cos_ref, sin_ref, qan_ref, wqn_ref, wqr_ref, qnn_ref, qnr_ref,
                       kvan_ref, knr_ref, wkb_ref, knn_ref, wvb_ref,
                       qcat_ref, kcat_ref, v_ref, c_ref, krope_ref, kinv_ref):
    qn, qr, cvec, krope, kraw = _mla_common(
        qlat_ref[...], kvr_ref[...], cos_ref[...], sin_ref[...], qan_ref[...], wqn_ref[...], wqr_ref[...],
        qnr_ref[...], kvan_ref[...], knr_ref[...], wkb_ref[...])
    invs, kinv8 = _k_inv_heads(kraw)
    kr64 = krope[:, :A_ROPE]
    for h in range(A_HEADS):
        qn_h = _rms(qn[:, h * A_NOPE:(h + 1) * A_NOPE], qnn_ref[...])
        qcat_ref[h] = jnp.concatenate([qn_h, qr[:, h * A_ROPE:(h + 1) * A_ROPE]], axis=-1).astype(BF16)
        kn_h = kraw[:, h * A_NOPE:(h + 1) * A_NOPE] * invs[h] * knn_ref[...]
        kcat_ref[h] = jnp.concatenate([kn_h, kr64], axis=-1).astype(BF16)
    v_ref[...] = jnp.dot(cvec.astype(BF16), wvb_ref[...], preferred_element_type=F32).astype(BF16)
    c_ref[...] = cvec
    krope_ref[...] = kr64
    kinv_ref[...] = kinv8


def _mla_sample_kernel(qlat_ref, kvr_ref, cos_ref, sin_ref, qan_ref, wqn_ref, wqr_ref, qnn_ref, qnr_ref,
                       kvan_ref, knr_ref, wkb_ref, knn_ref, wkbt_ref,
                       qa_ref, qr_ref, c_ref, krope_ref, kinv_ref):
    qn, qr, cvec, krope, kraw = _mla_common(
        qlat_ref[...], kvr_ref[...], cos_ref[...], sin_ref[...], qan_ref[...], wqn_ref[...], wqr_ref[...],
        qnr_ref[...], kvan_ref[...], knr_ref[...], wkb_ref[...])
    _, kinv8 = _k_inv_heads(kraw)
    for h in range(A_HEADS):
        qn_h = _rms(qn[:, h * A_NOPE:(h + 1) * A_NOPE], qnn_ref[...]) * knn_ref[...]
        qa_ref[:, h * KV_LORA:(h + 1) * KV_LORA] = jnp.dot(
            qn_h.astype(BF16), wkbt_ref[h], preferred_element_type=F32).astype(BF16)
    qr_ref[...] = qr.astype(BF16)
    c_ref[...] = cvec
    krope_ref[...] = krope[:, :A_ROPE]
    kinv_ref[...] = kinv8


def _mla_prep(q_lat, kvr, cos, sin, consts, tm, sample):
    T = q_lat.shape[0]
    rows = [q_lat, kvr, cos, sin]
    in_specs = []
    for a in rows:
        per = a.shape[0] // tm
        if a.shape[0] == T:
            in_specs.append(pl.BlockSpec((tm, a.shape[1]), lambda i: (i, 0)))
        else:
            in_specs.append(pl.BlockSpec((tm, a.shape[1]), lambda i, per=per: (i % per, 0)))
    in_specs += [_const_spec(a) for a in consts]
    row_out = lambda c: pl.BlockSpec((tm, c), lambda i: (i, 0))
    tail_shapes = [jax.ShapeDtypeStruct((T, KV_LORA), F32), jax.ShapeDtypeStruct((T, A_ROPE), F32),
                   jax.ShapeDtypeStruct((T, A_HEADS), F32)]
    tail_specs = [row_out(KV_LORA), row_out(A_ROPE), row_out(A_HEADS)]
    if sample:
        kern = _mla_sample_kernel
        out_shape = [jax.ShapeDtypeStruct((T, A_HEADS * KV_LORA), BF16),
                     jax.ShapeDtypeStruct((T, A_HEADS * A_ROPE), BF16)] + tail_shapes
        out_specs = [row_out(A_HEADS * KV_LORA), row_out(A_HEADS * A_ROPE)] + tail_specs
    else:
        kern = _mla_prompt_kernel
        dqk = A_NOPE + A_ROPE
        head_spec = pl.BlockSpec((A_HEADS, tm, dqk), lambda i: (0, i, 0))
        out_shape = [jax.ShapeDtypeStruct((A_HEADS, T, dqk), BF16), jax.ShapeDtypeStruct((A_HEADS, T, dqk), BF16),
                     jax.ShapeDtypeStruct((T, A_HEADS * A_VDIM), BF16)] + tail_shapes
        out_specs = [head_spec, head_spec, row_out(A_HEADS * A_VDIM)] + tail_specs
    return pl.pallas_call(kern, grid=(T // tm,), in_specs=in_specs, out_specs=out_specs, out_shape=out_shape,
                          compiler_params=_params(("parallel",)),
                          name="mla_prep_sample" if sample else "mla_prep_prompt")(*rows, *consts)


def _flash_kernel(q_ref, k_ref, v_ref, o_ref, *, tq):
    i = pl.program_id(2)
    q = q_ref[0]
    row = lax.broadcasted_iota(jnp.int32, (tq, tq), 0)
    col = lax.broadcasted_iota(jnp.int32, (tq, tq), 1)

    def step(j, carry, diagonal):
        m, l, acc = carry
        start = pl.multiple_of(j * tq, tq)
        k = k_ref[0, pl.ds(start, tq), :]
        v = v_ref[pl.ds(start, tq), :]
        s = lax.dot_general(q, k, (((1,), (1,)), ((), ())), preferred_element_type=F32) * ATT_SCALE
        if diagonal:
            s = jnp.where(col <= row, s, -jnp.inf)
        m_new = jnp.maximum(m, jnp.max(s, axis=-1, keepdims=True))
        alpha = jnp.exp(m - m_new)
        p = jnp.exp(s - m_new)
        l = alpha * l + jnp.sum(p, axis=-1, keepdims=True)
        acc = alpha * acc + jnp.dot(p.astype(BF16), v, preferred_element_type=F32)
        return m_new, l, acc

    init = (jnp.full((tq, 1), -jnp.inf, F32), jnp.zeros((tq, 1), F32), jnp.zeros((tq, A_VDIM), F32))
    carry = lax.fori_loop(0, i, lambda j, c: step(j, c, False), init)
    _, l, acc = step(i, carry, True)
    o_ref[...] = (acc / l).astype(o_ref.dtype)


def _flash_prompt(qcat, kcat, v, Bsz, S, tq):
    nq = S // tq
    dqk = A_NOPE + A_ROPE
    T = Bsz * S
    return pl.pallas_call(
        functools.partial(_flash_kernel, tq=tq),
        grid=(Bsz, A_HEADS, nq),
        in_specs=[pl.BlockSpec((1, tq, dqk), lambda b, h, i: (h, b * nq + i, 0)),
                  pl.BlockSpec((1, S, dqk), lambda b, h, i: (h, b, 0)),
                  pl.BlockSpec((S, A_VDIM), lambda b, h, i: (b, h))],
        out_specs=pl.BlockSpec((tq, A_VDIM), lambda b, h, i: (b * nq + i, h)),
        out_shape=jax.ShapeDtypeStruct((T, A_HEADS * A_VDIM), BF16),
        compiler_params=_params(("parallel", "parallel", "arbitrary")),
        name="flash_prompt")(qcat, kcat, v)


NEW_PAD = 16


def _decode_kernel(pt_ref, qa_ref, qr_ref, cn_ref, krn_ref, ksn_ref, ckv_hbm, krt_hbm, kst_hbm, o_ref,
                   ck_buf, kr_buf, ks_buf, sem, *, li, npg):
    b = pl.program_id(0)
    nseq = pl.num_programs(0)
    rows = qa_ref.shape[1]
    reps = rows // A_HEADS
    ngroups = pt_ref.shape[1] // npg

    def group_copies(seq, g, slot):
        cps = []
        for p in range(npg):
            pid = pt_ref[seq, g * npg + p]
            cps.append(pltpu.make_async_copy(ckv_hbm.at[li, pid], ck_buf.at[slot, p], sem.at[slot, 0]))
            cps.append(pltpu.make_async_copy(krt_hbm.at[li, pid], kr_buf.at[slot, p], sem.at[slot, 1]))
            cps.append(pltpu.make_async_copy(kst_hbm.at[li, pid], ks_buf.at[slot, p], sem.at[slot, 2]))
        return cps

    @pl.when(b == 0)
    def _():
        for cp in group_copies(0, 0, 0):
            cp.start()

    qa = qa_ref[0]
    qr = qr_ref[0]
    nt = (((1,), (1,)), ((), ()))

    def body(g, carry):
        m_old, l_old, acc_old = carry
        slot = (g & 1) if ngroups % 2 == 0 else ((b * ngroups + g) & 1)
        last = g + 1 == ngroups
        more = b + 1 < nseq
        nxt_seq = jnp.where(more, b + 1, b) if last else b
        nxt_g = jnp.where(more, 0, g) if last else g + 1
        for cp in group_copies(b, g, slot):
            cp.wait()
        for cp in group_copies(nxt_seq, nxt_g, 1 - slot):
            cp.start()
        m_run, l_run, acc = m_old, l_old, acc_old
        assert npg % 2 == 0
        nsplit = 2 if npg % 4 == 0 else 1
        per = npg // nsplit
        for part in range(nsplit):
            ss, ccs = [], []
            for p in range(part * per, (part + 1) * per, 2):
                cc = jnp.concatenate([ck_buf[slot, p], ck_buf[slot, p + 1]], axis=0).astype(BF16)
                ks2 = jnp.concatenate([ks_buf[slot, p], ks_buf[slot, p + 1]], axis=1)
                kr2 = jnp.concatenate([kr_buf[slot, p], kr_buf[slot, p + 1]], axis=1).astype(BF16)
                s = lax.dot_general(qa, cc, nt, preferred_element_type=F32)
                s = s * jnp.tile(ks2, (reps, 1))
                s = s + jnp.dot(qr, kr2, preferred_element_type=F32)
                ss.append(s * ATT_SCALE)
                ccs.append(cc)
            smax = ss[0]
            for s in ss[1:]:
                smax = jnp.maximum(smax, s)
            m_new = jnp.maximum(m_run, jnp.max(smax, axis=-1, keepdims=True))
            alpha = jnp.exp(m_run - m_new)
            acc = alpha * acc
            esum = None
            for s, cc in zip(ss, ccs):
                e = jnp.exp(s - m_new)
                esum = e if esum is None else esum + e
                acc = acc + jnp.dot(e.astype(BF16), cc, preferred_element_type=F32)
            l_run = alpha * l_run + jnp.sum(esum, axis=-1, keepdims=True)
            m_run = m_new
        return m_run, l_run, acc

    init = (jnp.full((rows, 1), -jnp.inf, F32), jnp.zeros((rows, 1), F32), jnp.zeros((rows, KV_LORA), F32))
    carry = init
    for g in range(ngroups):
        carry = body(g, carry)
    m1, l1, acc1 = carry

    @pl.when(b == nseq - 1)
    def _():
        idle = (ngroups & 1) if ngroups % 2 == 0 else 1 - ((b * ngroups + ngroups - 1) & 1)
        for cp in group_copies(b, ngroups - 1, idle):
            cp.wait()

    cn = cn_ref[0].astype(BF16)
    s = lax.dot_general(qa, cn, nt, preferred_element_type=F32) * ksn_ref[0]
    s = s + lax.dot_general(qr, krn_ref[0].astype(BF16), nt, preferred_element_type=F32)
    row = lax.broadcasted_iota(jnp.int32, s.shape, 0)
    col = lax.broadcasted_iota(jnp.int32, s.shape, 1)
    s = jnp.where(col <= (row >> 3), s * ATT_SCALE, -jnp.inf)
    m2 = jnp.maximum(m1, jnp.max(s, axis=-1, keepdims=True))
    a2 = jnp.exp(m1 - m2)
    e = jnp.exp(s - m2)
    l2 = a2 * l1 + jnp.sum(e, axis=-1, keepdims=True)
    acc2 = a2 * acc1 + jnp.dot(e.astype(BF16), cn, preferred_element_type=F32)
    o_ref[0] = acc2 / l2


def _decode_attend(page_table, qa, qr, cn, krn, ksn, cache_ckv, cache_krope_t, cache_kscale_t, li, npg):
    Bd, n_pages = page_table.shape
    rows = qa.shape[1]
    assert n_pages % npg == 0
    per_b = lambda b, pt: (b, 0, 0)
    hbm = pl.BlockSpec(memory_space=pl.ANY)
    in_specs = [pl.BlockSpec((1, rows, KV_LORA), per_b), pl.BlockSpec((1, rows, A_ROPE), per_b),
                pl.BlockSpec((1, NEW_PAD, KV_LORA), per_b), pl.BlockSpec((1, NEW_PAD, A_ROPE), per_b),
                pl.BlockSpec((1, rows, NEW_PAD), per_b), hbm, hbm, hbm]
    grid_spec = pltpu.PrefetchScalarGridSpec(
        num_scalar_prefetch=1, grid=(Bd,), in_specs=in_specs,
        out_specs=pl.BlockSpec((1, rows, KV_LORA), per_b),
        scratch_shapes=[pltpu.VMEM((2, npg, PAGE_SIZE, KV_LORA), F32), pltpu.VMEM((2, npg, A_ROPE, PAGE_SIZE), F32),
                        pltpu.VMEM((2, npg, A_HEADS, PAGE_SIZE), F32), pltpu.SemaphoreType.DMA((2, 3))])
    return pl.pallas_call(
        functools.partial(_decode_kernel, li=li, npg=npg), grid_spec=grid_spec,
        out_shape=jax.ShapeDtypeStruct((Bd, rows, KV_LORA), F32),
        compiler_params=_params(("arbitrary",)), name="decode_attend")(
            page_table, qa, qr, cn, krn, ksn, cache_ckv, cache_krope_t, cache_kscale_t)


def _tri_masks(c):
    r = lax.broadcasted_iota(jnp.int32, (c, c), 0)
    col = lax.broadcasted_iota(jnp.int32, (c, c), 1)
    return r, col


def _conv_silu(x_ref, buf0_ref, bufout_ref, xe_ref, w_ref, bias, j, c, b=0):
    @pl.when(j == 0)
    def _():
        xe_ref[b, 5:8, :] = buf0_ref[b]

    xe_ref[b, 8:8 + c, :] = x_ref[b]
    y = xe_ref[b, pl.ds(5, c), :] * w_ref[0:1, :]
    for k in range(1, 4):
        y = y + xe_ref[b, pl.ds(5 + k, c), :] * w_ref[k:k + 1, :]
    if bias is not None:
        y = y + bias
    bufout_ref[b] = xe_ref[b, c + 5:c + 8, :]
    tail = xe_ref[b, c:c + 8, :]
    xe_ref[b, 0:8, :] = tail
    return _silu(y)


def _ssd_kernel(xbc_ref, z_ref, small_ref, smallt_ref, buf0_ref, s0_ref, cw_ref, cb_ref, dtb_ref, alog_ref,
                dtbc_ref, alogc_ref, dskip_ref, ng_ref, y_ref, bufout_ref, sout_ref, xe_ref, *, c, nb):
    j = pl.program_id(1)

    @pl.when(j == 0)
    def _():
        sout_ref[...] = s0_ref[...]

    gw = M_GROUPS * M_STATE
    rep = M_HEADS // M_GROUPS
    r, col = _tri_masks(c)
    tri = col <= r
    xs, bm, cm, dt, dtt, cum, cumt, cb = [], [], [], [], [], [], [], []
    for b in range(nb):
        xbc = _conv_silu(xbc_ref, buf0_ref, bufout_ref, xe_ref, cw_ref, cb_ref[...], j, c, b)
        xs.append(xbc[:, :M_WIDTH])
        bm.append(xbc[:, M_WIDTH:M_WIDTH + gw])
        cm.append(xbc[:, M_WIDTH + gw:])
        dt.append(_softplus(small_ref[b] + dtb_ref[...]))
        dtt.append(_softplus(smallt_ref[b, 0][:M_HEADS] + dtbc_ref[...]))
        cum.append(_dot_exact_left(tri, dt[b] * (-jnp.exp(alog_ref[...]))))
        cumt.append(_dot_exact_right(dtt[b] * (-jnp.exp(alogc_ref[...])), r <= col))
        cb.append([_mm_nt(cm[b][:, g * M_STATE:(g + 1) * M_STATE], bm[b][:, g * M_STATE:(g + 1) * M_STATE])
                   for g in range(M_GROUPS)])
    chains = [(b, h) for b in range(nb) for h in range(M_HEADS)]
    grp = lambda t, b, h: t[b][:, (h // rep) * M_STATE:(h // rep + 1) * M_STATE]
    cc = [cum[b][:, h:h + 1] for b, h in chains]
    mmat = [cb[b][h // rep] * jnp.exp(jnp.where(tri, cc[i] - cumt[b][h:h + 1, :], -jnp.inf)) * dtt[b][h:h + 1, :]
            for i, (b, h) in enumerate(chains)]
    xh = [xs[b][:, h * M_HEADDIM:(h + 1) * M_HEADDIM] for b, h in chains]
    s_old = [sout_ref[b, h] for b, h in chains]
    y = [_mm(mmat[i], xh[i]) + _mm_nt(grp(cm, b, h), s_old[i]) * jnp.exp(cc[i])
         for i, (b, h) in enumerate(chains)]
    for i, (b, h) in enumerate(chains):
        clast = cum[b][c - 1:c, h:h + 1]
        to_end = jnp.exp(clast - cc[i]) * dt[b][:, h:h + 1]
        sout_ref[b, h] = s_old[i] * jnp.exp(clast) + _mm_tn(xh[i] * to_end, grp(bm, b, h))
    for b in range(nb):
        yb = jnp.concatenate(y[b * M_HEADS:(b + 1) * M_HEADS], axis=-1) + xs[b] * dskip_ref[...]
        y_ref[b] = _rms(yb * _silu(z_ref[b]), ng_ref[...])


def _ssd_scan(xbc, z, small, smallt, buf0, s0, consts, c, nb):
    Bsz, L, _ = xbc.shape
    nc = L // c
    assert Bsz % nb == 0
    blk = lambda w: pl.BlockSpec((nb, c, w), lambda b, j: (b, j, 0))
    buf_spec = pl.BlockSpec((nb, M_CONV - 1, M_CONV_CH), lambda b, j: (b, 0, 0))
    state_spec = pl.BlockSpec((nb, M_HEADS, M_HEADDIM, M_STATE), lambda b, j: (b, 0, 0, 0))
    in_specs = [blk(M_CONV_CH), blk(M_WIDTH), blk(SMALL_W),
                pl.BlockSpec((nb, 1, 16, c), lambda b, j: (b, j, 0, 0)), buf_spec, state_spec]
    in_specs += [_const_spec(a) for a in consts]
    out_shape = [jax.ShapeDtypeStruct((Bsz, L, M_WIDTH), F32),
                 jax.ShapeDtypeStruct((Bsz, M_CONV - 1, M_CONV_CH), F32),
                 jax.ShapeDtypeStruct((Bsz, M_HEADS, M_HEADDIM, M_STATE), F32)]
    return pl.pallas_call(
        functools.partial(_ssd_kernel, c=c, nb=nb), grid=(Bsz // nb, nc), in_specs=in_specs,
        out_specs=[blk(M_WIDTH), buf_spec, state_spec],
        out_shape=out_shape, scratch_shapes=[pltpu.VMEM((nb, c + 8, M_CONV_CH), F32)],
        compiler_params=_params(("parallel", "arbitrary")), name="ssd_scan")(
            xbc, z, small, smallt, buf0, s0, *consts)


def _unit_lower_inv_many(amats, c, r, col):
    eye = (r == col).astype(F32)
    same8 = (r >> 3) == (col >> 3)
    a8 = [jnp.where(same8, a, 0.0) for a in amats]
    x = [eye - a for a in a8]
    p = [_dot3(a, a) for a in a8]
    x = [xi + _dot3(xi, pi) for xi, pi in zip(x, p)]
    p = [_dot3(pi, pi) for pi in p]
    x = [xi + _dot3(xi, pi) for xi, pi in zip(x, p)]
    b = 8
    while b < c:
        sh = int(math.log2(b))
        pick = ((r >> (sh + 1)) == (col >> (sh + 1))) & ((r >> sh) != (col >> sh))
        low = [jnp.where(pick, a, 0.0) for a in amats]
        t = [_dot3(xi, li) for xi, li in zip(x, low)]
        x = [xi - _dot3(ti, xi) for xi, ti in zip(x, t)]
        b *= 2
    return x


def _gdn_kernel(qkv_ref, z_ref, small_ref, smallt_ref, buf0_ref, s0_ref, cw_ref, gdt_ref, galog_ref,
                gdtc_ref, galogc_ref, ng_ref, o_ref, bufout_ref, sout_ref, xe_ref, *, c, nb):
    j = pl.program_id(1)

    @pl.when(j == 0)
    def _():
        sout_ref[...] = s0_ref[...]

    nq = G_HEADS * G_DK
    r, col = _tri_masks(c)
    incl = col <= r
    strict = col < r
    chains = [(b, h) for b in range(nb) for h in range(G_HEADS)]
    qkv, small, gcum, gcumt = [], [], [], []
    for b in range(nb):
        qkv.append(_conv_silu(qkv_ref, buf0_ref, bufout_ref, xe_ref, cw_ref, None, j, c, b))
        small.append(small_ref[b])
        gall = -jnp.exp(galog_ref[...]) * _softplus(small[b] + gdt_ref[...])
        gallt = -jnp.exp(galogc_ref[...]) * _softplus(smallt_ref[b, 0] + gdtc_ref[...])
        gcum.append(_dot_exact_left(incl, gall))
        gcumt.append(_dot_exact_right(gallt, r <= col))
    q, k, v, beta, gc, decay = [], [], [], [], [], []
    for b, h in chains:
        qh = qkv[b][:, h * G_DK:(h + 1) * G_DK]
        kh = qkv[b][:, nq + h * G_DK:nq + (h + 1) * G_DK]
        q.append(qh * lax.rsqrt(jnp.sum(qh * qh, axis=-1, keepdims=True) + EPS) * (G_DK ** -0.5))
        k.append(kh * lax.rsqrt(jnp.sum(kh * kh, axis=-1, keepdims=True) + EPS))
        v.append(qkv[b][:, 2 * nq + h * G_DV:2 * nq + (h + 1) * G_DV])
        beta.append(jax.nn.sigmoid(small[b][:, G_B_COL + h:G_B_COL + h + 1]))
        gc.append(gcum[b][:, G_A_COL + h:G_A_COL + h + 1])
        gr = gcumt[b][G_A_COL + h:G_A_COL + h + 1, :]
        decay.append(jnp.exp(jnp.where(incl, gc[-1] - gr, -jnp.inf)))
    n = len(chains)
    kb = [k[i] * beta[i] for i in range(n)]
    amat = [jnp.where(strict, _mm_nt(kb[i], k[i]) * decay[i], 0.0) for i in range(n)]
    tmat = _unit_lower_inv_many(amat, c, r, col)
    u = [_mm(tmat[i], v[i] * beta[i]) for i in range(n)]
    w = [_mm(tmat[i], kb[i] * jnp.exp(gc[i])) for i in range(n)]
    attn = [jnp.where(incl, _mm_nt(q[i], k[i]) * decay[i], 0.0) for i in range(n)]
    s_old = [sout_ref[b, h] for b, h in chains]
    v_new = [u[i] - _mm(w[i], s_old[i]) for i in range(n)]
    o = [_mm(q[i] * jnp.exp(gc[i]), s_old[i]) + _mm(attn[i], v_new[i]) for i in range(n)]
    for i, (b, h) in enumerate(chains):
        glast = gcum[b][c - 1:c, G_A_COL + h:G_A_COL + h + 1]
        sout_ref[b, h] = s_old[i] * jnp.exp(glast) + _mm_tn(k[i] * jnp.exp(glast - gc[i]), v_new[i])
    for b in range(nb):
        z = z_ref[b]
        outs = [_rms(o[b * G_HEADS + h], ng_ref[...]) * _silu(z[:, h * G_DV:(h + 1) * G_DV])
                for h in range(G_HEADS)]
        o_ref[b] = jnp.concatenate(outs, axis=-1)


def _gdn_scan(qkv, z, small, smallt, buf0, s0, consts, c, nb):
    Bsz, L, _ = qkv.shape
    nc = L // c
    assert Bsz % nb == 0
    blk = lambda w: pl.BlockSpec((nb, c, w), lambda b, j: (b, j, 0))
    buf_spec = pl.BlockSpec((nb, G_CONV - 1, G_CONV_CH), lambda b, j: (b, 0, 0))
    state_spec = pl.BlockSpec((nb, G_HEADS, G_DK, G_DV), lambda b, j: (b, 0, 0, 0))
    in_specs = [blk(G_CONV_CH), blk(G_WIDTH), blk(SMALL_W),
                pl.BlockSpec((nb, 1, 16, c), lambda b, j: (b, j, 0, 0)), buf_spec, state_spec]
    in_specs += [_const_spec(a) for a in consts]
    out_shape = [jax.ShapeDtypeStruct((Bsz, L, G_WIDTH), F32),
                 jax.ShapeDtypeStruct((Bsz, G_CONV - 1, G_CONV_CH), F32),
                 jax.ShapeDtypeStruct((Bsz, G_HEADS, G_DK, G_DV), F32)]
    return pl.pallas_call(
        functools.partial(_gdn_kernel, c=c, nb=nb), grid=(Bsz // nb, nc), in_specs=in_specs,
        out_specs=[blk(G_WIDTH), buf_spec, state_spec],
        out_shape=out_shape, scratch_shapes=[pltpu.VMEM((nb, c + 8, G_CONV_CH), F32)],
        compiler_params=_params(("parallel", "arbitrary")), name="gdn_scan")(
            qkv, z, small, smallt, buf0, s0, *consts)


def _out_proj_prompt_body(rows, consts, outs):
    a, m, g, x = rows
    wa, wm, wg = consts
    outs[0][...] = x[...] + _mm(a[...], wa[...]) + _mm(m[...], wm[...]) + _mm(g[...], wg[...])


def _out_proj_sample_body(rows, consts, outs):
    olat, m, g, x = rows
    wvb, wa, wm, wg = consts
    heads = [_mm(olat[:, h * KV_LORA:(h + 1) * KV_LORA], wvb[h]) for h in range(A_HEADS)]
    a = jnp.concatenate(heads, axis=-1)
    outs[0][...] = x[...] + _mm(a, wa[...]) + _mm(m[...], wm[...]) + _mm(g[...], wg[...])


def _top_values(s, k):
    out = []
    for it in range(k):
        m = jnp.max(s, axis=0, keepdims=True)
        out.append(m)
        if it + 1 < k:
            s = jnp.where(s == m, -jnp.inf, s)
    return out


def _peer_query_kernel(x_ref, ln2_ref, wqt_ref, keys_ref, h2t_ref, th1_ref, e1_ref, s2_ref, e2_ref):
    h2t = _rms(x_ref[...], ln2_ref[...]).T.astype(BF16)
    h2t_ref[...] = h2t
    qt = jnp.dot(wqt_ref[...], h2t, preferred_element_type=F32).astype(BF16)
    half = P_DKEY // 2
    for h in range(P_HEADS):
        s1 = jnp.dot(keys_ref[2 * h], qt[(2 * h) * half:(2 * h + 1) * half], preferred_element_type=F32)
        s2 = jnp.dot(keys_ref[2 * h + 1], qt[(2 * h + 1) * half:(2 * h + 2) * half], preferred_element_type=F32)
        v1 = _top_values(s1, P_TOPK)
        v2 = _top_values(s2, P_TOPK)
        v1m = jnp.concatenate(v1, axis=0)
        v2m = jnp.concatenate(v2, axis=0)
        hk = P_TOPK // 2
        cand = jnp.concatenate([v1[0] + v2m] + [v1[a] + v2m[:hk] for a in range(1, hk)] + [v1m[hk:] + v2[0]],
                               axis=0)
        best = _top_values(cand, P_TOPK)
        zsum = jnp.ones_like(best[0])
        for b in best[1:]:
            zsum = zsum + jnp.exp(b - best[0])
        th1_ref[h] = best[P_TOPK - 1] - s1
        s2_ref[h] = s2
        e1_ref[h] = jnp.exp(s1 - v1[0])
        e2_ref[h] = (jnp.exp(s2 - v2[0]) * (0.5 / zsum)).astype(BF16)


def _peer_query(x, ln2, wqt, keys, tm):
    T, D = x.shape
    hk = pl.BlockSpec((P_HEADS, P_NKEYS, tm), lambda i: (0, 0, i))
    hk_shape = jax.ShapeDtypeStruct((P_HEADS, P_NKEYS, T), F32)
    return pl.pallas_call(
        _peer_query_kernel, grid=(T // tm,),
        in_specs=[pl.BlockSpec((tm, D), lambda i: (i, 0)), _const_spec(ln2), _const_spec(wqt), _const_spec(keys)],
        out_specs=[pl.BlockSpec((D, tm), lambda i: (0, i)), hk, hk, hk, hk],
        out_shape=[jax.ShapeDtypeStruct((D, T), BF16), hk_shape, hk_shape, hk_shape,
                   jax.ShapeDtypeStruct((P_HEADS, P_NKEYS, T), BF16)],
        compiler_params=_params(("parallel",)), name="peer_query")(x, ln2, wqt, keys)


def _gelu_x2(x):
    return x * (1.0 + lax.erf(x * (2.0 ** -0.5)))


def _peer_gate(th1_ref, e1_ref, s2_ref, e2_ref, i1):
    g = None
    for h in range(P_HEADS):
        keep = s2_ref[h] >= th1_ref[h, pl.ds(i1, 1), :]
        w = jnp.where(keep, e2_ref[h] * e1_ref[h, pl.ds(i1, 1), :].astype(BF16), jnp.zeros((), BF16))
        g = w if g is None else g + w
    return g


def _peer_expert_kernel(h2t_ref, th1_ref, e1_ref, s2_ref, e2_ref, u_ref, vt_ref, y_ref, ga_ref, gb_ref, *, ec):
    j = pl.program_id(1)
    nj = pl.num_programs(1)
    half = ec // 2
    nsub = half // P_NKEYS

    def fill(g_ref, i1_base):
        for r in range(nsub):
            g_ref[r * P_NKEYS:(r + 1) * P_NKEYS, :] = _peer_gate(th1_ref, e1_ref, s2_ref, e2_ref, i1_base + r)

    @pl.when(j == 0)
    def _():
        fill(ga_ref, 0)
        y_ref[...] = jnp.zeros(y_ref.shape, F32)

    pt_a = jnp.dot(u_ref[:half, :], h2t_ref[...], preferred_element_type=F32)
    fill(gb_ref, j * 2 * nsub + nsub)
    act_a = _gelu_x2(pt_a).astype(BF16) * ga_ref[...]
    pt_b = jnp.dot(u_ref[half:, :], h2t_ref[...], preferred_element_type=F32)
    fill(ga_ref, jnp.minimum(j + 1, nj - 1) * 2 * nsub)
    act_b = _gelu_x2(pt_b).astype(BF16) * gb_ref[...]
    act = jnp.concatenate([act_a, act_b], axis=0)
    y_ref[...] += jnp.dot(vt_ref[...], act, preferred_element_type=F32)


def _peer_experts(h2t, th1, e1, s2, e2, u, vt, tm, ec):
    D, T = h2t.shape
    ne = u.shape[0]
    hk = pl.BlockSpec((P_HEADS, P_NKEYS, tm), lambda i, j: (0, 0, i))
    return pl.pallas_call(
        functools.partial(_peer_expert_kernel, ec=ec), grid=(T // tm, ne // ec),
        in_specs=[pl.BlockSpec((D, tm), lambda i, j: (0, i)), hk, hk, hk, hk,
                  pl.BlockSpec((ec, D), lambda i, j: (j, 0)), pl.BlockSpec((D, ec), lambda i, j: (0, j))],
        out_specs=pl.BlockSpec((D, tm), lambda i, j: (0, i)), out_shape=jax.ShapeDtypeStruct((D, T), F32),
        scratch_shapes=[pltpu.VMEM((ec // 2, tm), BF16), pltpu.VMEM((ec // 2, tm), BF16)],
        compiler_params=_params(("parallel", "arbitrary")), name="peer_experts")(
            h2t, th1, e1, s2, e2, u, vt)


def _peer_finish_kernel(x_ref, yt_ref, o_ref):
    o_ref[...] = x_ref[...] + yt_ref[...].T


def _peer_finish(x, yt, tm):
    T, D = x.shape
    tok = pl.BlockSpec((tm, D), lambda i: (i, 0))
    return pl.pallas_call(
        _peer_finish_kernel, grid=(T // tm,), in_specs=[tok, pl.BlockSpec((D, tm), lambda i: (0, i))],
        out_specs=tok, out_shape=jax.ShapeDtypeStruct((T, D), F32),
        compiler_params=_params(("parallel",)), name="peer_finish")(x, yt)


def _row(v, width=None):
    v = v.astype(F32).reshape(1, -1)
    if width is not None and v.shape[1] < width:
        v = jnp.pad(v, ((0, 0), (0, width - v.shape[1])))
    return v


def _layer_weights(i, ln1, w_in, w_out, ln2, q_a_norm, w_qb, kv_a_norm, w_kb, w_vb, qn_nope, qn_rope, kn_nope,
                   kn_rope, m_conv_w, m_conv_b, m_dt_bias, m_A_log, m_D, m_norm, g_conv_w, g_dt_bias, g_A_log,
                   g_norm, p_wq, p_keys, p_u, p_v):
    D = w_in.shape[1]
    o = np.cumsum((0,) + IN_SIZES)
    wi = w_in[i]
    seg = lambda k: wi[:, o[k]:o[k + 1]]
    zeros = lambda n: jnp.zeros((D, n), F32)
    w_kvr = jnp.concatenate([seg(1), seg(2), zeros(LANES - A_ROPE)], axis=1)
    w_small = jnp.concatenate([seg(5), seg(8), seg(9), zeros(SMALL_W - M_HEADS - 2 * G_HEADS)], axis=1)
    in_ws = [w.astype(BF16) for w in (seg(0), w_kvr, seg(3), seg(4), seg(6), seg(7), w_small)]
    wqb = w_qb[i].reshape(Q_LORA, A_HEADS, A_NOPE + A_ROPE)
    w_qn = wqb[:, :, :A_NOPE].reshape(Q_LORA, A_HEADS * A_NOPE).astype(BF16)
    w_qr = wqb[:, :, A_NOPE:].reshape(Q_LORA, A_HEADS * A_ROPE).astype(BF16)
    wkb = w_kb[i]
    wvb = w_vb[i]
    mla_common = [_row(q_a_norm[i]), w_qn, w_qr, _row(qn_nope[i]), _row(jnp.tile(qn_rope[i], A_HEADS)),
                  _row(kv_a_norm[i]), _row(kn_rope[i], LANES), wkb.reshape(KV_LORA, A_HEADS * A_NOPE).astype(BF16),
                  _row(kn_nope[i])]
    mla_prompt = mla_common + [wvb.reshape(KV_LORA, A_HEADS * A_VDIM).astype(BF16)]
    mla_sample = mla_common + [jnp.transpose(wkb, (1, 2, 0)).astype(BF16)]
    wvb_heads = jnp.transpose(wvb, (1, 0, 2)).astype(BF16)
    col16 = lambda v, off: jnp.zeros((16, 1), F32).at[off:off + v.shape[0], 0].set(v.astype(F32))
    ssd = [m_conv_w[i].astype(F32), _row(m_conv_b[i]), _row(m_dt_bias[i], SMALL_W), _row(m_A_log[i], SMALL_W),
           m_dt_bias[i].astype(F32).reshape(M_HEADS, 1), m_A_log[i].astype(F32).reshape(M_HEADS, 1),
           _row(jnp.repeat(m_D[i], M_HEADDIM)), _row(m_norm[i])]
    pad_a = lambda v: jnp.zeros((1, SMALL_W), F32).at[0, G_A_COL:G_A_COL + G_HEADS].set(v.astype(F32))
    gdn = [g_conv_w[i].astype(F32), pad_a(g_dt_bias[i]), pad_a(g_A_log[i]),
           col16(g_dt_bias[i], G_A_COL), col16(g_A_log[i], G_A_COL), _row(g_norm[i])]
    wo = w_out[i].astype(BF16)
    aw = A_HEADS * A_VDIM
    out_ws = [wo[:aw], wo[aw:aw + M_WIDTH], wo[aw + M_WIDTH:]]
    peer = dict(ln2=_row(ln2[i]), wqt=p_wq[i].T.astype(BF16),
                keys=p_keys[i].reshape(2 * P_HEADS, P_NKEYS, P_DKEY // 2).astype(BF16),
                u=p_u[i].astype(BF16), vt=p_v[i].T.astype(BF16))
    return dict(ln1=_row(ln1[i]), in_ws=in_ws, mla_prompt=mla_prompt, mla_sample=mla_sample, wvb_heads=wvb_heads,
                ssd=ssd, gdn=gdn, out_ws=out_ws, peer=peer)


def _rope_tables(pos):
    half = A_ROPE // 2
    inv = ROPE_THETA ** (-jnp.arange(half, dtype=F32) / half)
    ang = pos.astype(F32)[:, None] * inv[None, :]
    cos, sin = jnp.cos(ang), jnp.sin(ang)
    reps = LANES // A_ROPE
    cos_t = jnp.tile(jnp.concatenate([cos, cos], axis=1), (1, reps))
    sin_t = jnp.tile(jnp.concatenate([-sin, sin], axis=1), (1, reps))
    return cos_t, sin_t


def _small_t(small, Bsz, L, c):
    s = small[:, :16].reshape(Bsz, L // c, c, 16)
    return jnp.swapaxes(s, 2, 3)


def _pick_tile(T, cap):
    t = cap
    while T % t:
        t //= 2
    return t


def _token_layer(x, Bsz, L, cos, sin, W, m_buf, m_s, g_buf, g_s, attend, sample):
    T = x.shape[0]
    tm = _pick_tile(T, 256)
    q_lat, kvr, m_z, m_xbc, g_qkv, g_z, small = _in_proj(x, W["ln1"], W["in_ws"], tm)
    mla = _mla_prep(q_lat, kvr, cos, sin, W["mla_sample"] if sample else W["mla_prompt"], tm, sample)
    a_out, cvec, krope, kinv = attend(mla)
    c = CHUNK if L % CHUNK == 0 else L
    smallt = _small_t(small, Bsz, L, c)
    r3 = lambda a: a.reshape(Bsz, L, a.shape[-1])
    m_y, m_buf, m_s = _ssd_scan(r3(m_xbc), r3(m_z), r3(small), smallt, m_buf, m_s, W["ssd"], c,
                                 _pick_tile(Bsz, 4 if sample else 2))
    g_y, g_buf, g_s = _gdn_scan(r3(g_qkv), r3(g_z), r3(small), smallt, g_buf, g_s, W["gdn"], c,
                                 _pick_tile(Bsz, 4))
    m_y = m_y.reshape(T, M_WIDTH)
    g_y = g_y.reshape(T, G_WIDTH)
    D = x.shape[1]
    if sample:
        (x,) = _rowwise(_out_proj_sample_body, [a_out, m_y, g_y, x], [W["wvb_heads"]] + W["out_ws"],
                        [(D, F32)], tm, "out_proj_sample")
    else:
        (x,) = _rowwise(_out_proj_prompt_body, [a_out, m_y, g_y, x], W["out_ws"], [(D, F32)], tm, "out_proj_prompt")
    pw = W["peer"]
    h2t, th1, e1, s2, e2 = _peer_query(x, pw["ln2"], pw["wqt"], pw["keys"], tm)
    tp = _pick_tile(T, 512)
    yt = _peer_experts(h2t, th1, e1, s2, e2, pw["u"], pw["vt"], tp, 1024)
    x = _peer_finish(x, yt, tp)
    return x, (cvec, krope, kinv, m_s, m_buf, g_s, g_buf)


def kernel(x_prompt, x_sample, cache_ckv, cache_krope, cache_kscale, state_ssm, state_ssm_conv, state_gdn, state_gdn_conv, page_table, ln1, w_in, w_out, ln2, q_a_norm, w_qb, kv_a_norm, w_kb, w_vb, qn_nope, qn_rope, kn_nope, kn_rope, m_conv_w, m_conv_b, m_dt_bias, m_A_log, m_D, m_norm, g_conv_w, g_dt_bias, g_A_log, g_norm, p_wq, p_keys, p_u, p_v):
    B, S, D = x_prompt.shape
    Bd, Ld, _ = x_sample.shape
    depth = ln1.shape[0]
    n_pages = page_table.shape[1]
    past = n_pages * PAGE_SIZE
    cos_p, sin_p = _rope_tables(jnp.arange(S))
    cos_s, sin_s = _rope_tables(past + jnp.arange(Ld))
    cos_s, sin_s = jnp.tile(cos_s, (Bd, 1)), jnp.tile(sin_s, (Bd, 1))
    krope_t = jnp.swapaxes(cache_krope, -1, -2)
    kscale_t = jnp.swapaxes(cache_kscale, -1, -2)
    npg = _pick_tile(n_pages, 32)
    tq = _pick_tile(S, 512)
    xp = x_prompt.reshape(B * S, D)
    xs = x_sample.reshape(Bd * Ld, D)
    new_p = [[] for _ in range(7)]
    new_s = [[] for _ in range(7)]
    weights = (ln1, w_in, w_out, ln2, q_a_norm, w_qb, kv_a_norm, w_kb, w_vb, qn_nope, qn_rope, kn_nope, kn_rope,
               m_conv_w, m_conv_b, m_dt_bias, m_A_log, m_D, m_norm, g_conv_w, g_dt_bias, g_A_log, g_norm,
               p_wq, p_keys, p_u, p_v)
    for i in range(depth):
        W = _layer_weights(i, *weights)

        def attend_prompt(mla):
            qcat, kcat, v, cvec, krope, kinv = mla
            return _flash_prompt(qcat, kcat, v, B, S, tq), cvec, krope, kinv

        def attend_sample(mla, i=i):
            qa, qr, cvec, krope, kinv = mla
            rows = Ld * A_HEADS
            padk = lambda a: jnp.pad(a.reshape(Bd, Ld, a.shape[-1]), ((0, 0), (0, NEW_PAD - Ld), (0, 0)))
            ksn = jnp.swapaxes(kinv.reshape(Bd, Ld, A_HEADS), 1, 2)
            ksn = jnp.pad(jnp.tile(ksn, (1, Ld, 1)), ((0, 0), (0, 0), (0, NEW_PAD - Ld)))
            o_lat = _decode_attend(page_table, qa.reshape(Bd, rows, KV_LORA), qr.reshape(Bd, rows, A_ROPE),
                                   padk(cvec), padk(krope), ksn, cache_ckv, krope_t, kscale_t, i, npg)
            return o_lat.reshape(Bd * Ld, A_HEADS * KV_LORA), cvec, krope, kinv

        zeros = lambda *s: jnp.zeros(s, F32)
        xp, st_p = _token_layer(xp, B, S, cos_p, sin_p, W, zeros(B, M_CONV - 1, M_CONV_CH),
                                zeros(B, M_HEADS, M_HEADDIM, M_STATE), zeros(B, G_CONV - 1, G_CONV_CH),
                                zeros(B, G_HEADS, G_DK, G_DV), attend_prompt, False)
        xs, st_s = _token_layer(xs, Bd, Ld, cos_s, sin_s, W, state_ssm_conv[i], state_ssm[i],
                                state_gdn_conv[i], state_gdn[i], attend_sample, True)
        for lst, val in zip(new_p, st_p):
            lst.append(val)
        for lst, val in zip(new_s, st_s):
            lst.append(val)

    def pack(vals, Bsz, L):
        cvec, krope, kinv, m_s, m_buf, g_s, g_buf = [jnp.stack(v) for v in vals]
        r = lambda a: a.reshape(depth, Bsz, L, a.shape[-1])
        return r(cvec), r(krope), r(kinv), m_s, m_buf, g_s, g_buf

    return (xp.reshape(B, S, D), xs.reshape(Bd, Ld, D)) + pack(new_p, B, S) + pack(new_s, Bd, Ld)
```
